```python
import math
import jax, jax.numpy as jnp
from jax import lax
import numpy as np

D_MODEL = 2048
BATCH = 1
SEQ = 8192
DEPTH = 2
DEC_BATCH = 32
DEC_SEQ = 1
PAST_LEN = 8192
PAGE_SIZE = 128

F32 = jnp.float32
A_HEADS = 8
A_DK = 128
A_DV = 128
A_CHUNK = 64
A_WIDTH_K = A_HEADS * A_DK
A_WIDTH_V = A_HEADS * A_DV
NSA_HEADS = 16
NSA_KV = 4
NSA_HD = 64
NSA_WIDTH_Q = NSA_HEADS * NSA_HD
NSA_WIDTH_KV = NSA_KV * NSA_HD
CMP_BLOCK = 32
CMP_STRIDE = 16
CMP_HIDDEN = 128
SEL_BLOCK = 64
N_SEL = 16
WINDOW = 512
Q_BLOCK = 128
ROPE_THETA = 10000.0
N_EXPERTS = 16
N_GROUPS = 4
EXPERTS_PER_GROUP = N_EXPERTS // N_GROUPS
TOP_K = 2
D_FF_EXPERT = 1024
ALPHA = (2 * DEPTH) ** 0.25
BETA = (8 * DEPTH) ** -0.25
LN_EPS = 1e-5
SPLIT_SIZES = (A_WIDTH_K, A_WIDTH_K, A_WIDTH_V, A_WIDTH_V, NSA_WIDTH_Q, 6 * NSA_WIDTH_KV, 3 * NSA_HEADS, D_MODEL, D_MODEL)
D_IN = sum(SPLIT_SIZES)

kernel_name = 'hgrn2_nsa_gated_moe_decoder_step'


def layer_norm(x, g, b):
    xf = x.astype(F32)
    mu = xf.mean(-1, keepdims=True)
    var = jnp.square(xf - mu).mean(-1, keepdims=True)
    return ((xf - mu) * lax.rsqrt(var + LN_EPS) * g.astype(F32) + b.astype(F32)).astype(x.dtype)


def project(x, w_in):
    z = jnp.einsum('bld,de->ble', x, w_in)
    pts = np.cumsum(SPLIT_SIZES)[:-1].tolist()
    return jnp.split(z, pts, axis=-1)


def rope(x, pos):
    half = x.shape[-1] // 2
    inv = jnp.power(ROPE_THETA, -jnp.arange(half, dtype=F32) / half)
    ang = pos.astype(F32)[:, None] * inv[None, :]
    cos = jnp.cos(ang)[None, :, None, :]
    sin = jnp.sin(ang)[None, :, None, :]
    x1 = x[..., :half].astype(F32)
    x2 = x[..., half:].astype(F32)
    return jnp.concatenate([x1 * cos - x2 * sin, x2 * cos + x1 * sin], axis=-1).astype(x.dtype)


def masked_softmax(s, mask):
    s = jnp.where(mask, s, -1e30)
    e = jnp.exp(s - s.max(-1, keepdims=True)) * mask
    d = e.sum(-1, keepdims=True)
    return e / jnp.where(d > 0, d, 1.0)


def hgrn_features(qa, fa, ia, lb):
    B, L, _ = qa.shape
    q = jax.nn.silu(qa.astype(F32)).reshape(B, L, A_HEADS, A_DK)
    lbf = lb.astype(F32)
    log_f = jnp.logaddexp(jnp.log(lbf), jnp.log1p(-lbf) + jax.nn.log_sigmoid(fa.astype(F32)))
    log_f = log_f.reshape(B, L, A_HEADS, A_DK)
    k = -jnp.expm1(log_f)
    v = ia.astype(F32).reshape(B, L, A_HEADS, A_DV)
    return q, k, v, log_f


def hgrn_scan(q, k, v, log_f, s0):
    B, L, H, DK = q.shape
    DV = v.shape[-1]
    C = math.gcd(L, A_CHUNK)
    n = L // C

    def to_chunks(a):
        return jnp.moveaxis(a.reshape(B, n, C, H, a.shape[-1]), 1, 0)

    causal = jnp.tril(jnp.ones((C, C), dtype=bool))[None, :, :, None, None]

    def step(S, inp):
        qc, kc, vc, gc = inp
        b = jnp.cumsum(gc, axis=1)
        diff = b[:, :, None] - b[:, None, :]
        decay = jnp.exp(jnp.where(causal, diff, -jnp.inf))
        A = jnp.einsum('bthd,btshd,bshd->bhts', qc, decay, kc)
        o = jnp.einsum('bhts,bshv->bthv', A, vc) + jnp.einsum('bthd,bhdv->bthv', qc * jnp.exp(b), S)
        b_last = b[:, -1]
        S_new = jnp.exp(b_last)[..., None] * S + jnp.einsum('bshd,bshv->bhdv', kc * jnp.exp(b_last[:, None] - b), vc)
        return S_new, o

    S, o = lax.scan(step, s0, (to_chunks(q), to_chunks(k), to_chunks(v), to_chunks(log_f)))
    o = jnp.moveaxis(o, 0, 1).reshape(B, L, H, DV)
    return o, S


def hgrn_output(o, ga, gnorm):
    B, L = o.shape[:2]
    o = o * lax.rsqrt(jnp.mean(o * o, axis=-1, keepdims=True) + 1e-6)
    o = o.reshape(B, L, A_WIDTH_V) * gnorm.astype(F32) * jax.nn.silu(ga.astype(F32))
    return o.astype(ga.dtype)


def nsa_features(qb, kvb, gb, pos):
    B, L, _ = qb.shape
    q = qb.reshape(B, L, NSA_HEADS, NSA_HD)
    kv = kvb.reshape(B, L, 6, NSA_KV, NSA_HD)
    k_c, v_c, k_s, v_s, k_w, v_w = [kv[:, :, i] for i in range(6)]
    q_rot = rope(q, pos)
    k_s = rope(k_s, pos)
    k_w = rope(k_w, pos)
    gates = jax.nn.sigmoid(gb.astype(F32)).reshape(B, L, NSA_HEADS, 3)
    return q, q_rot, (k_c, v_c, k_s, v_s), (k_w, v_w), gates


def compress(x, pe, w1, w2):
    B, L, G, hd = x.shape
    r = CMP_BLOCK // CMP_STRIDE
    nh = L // CMP_STRIDE
    nc = nh - r + 1
    pieces = x[:, :nh * CMP_STRIDE].reshape(B, nh, CMP_STRIDE, G, hd)
    blocks = jnp.concatenate([pieces[:, j:j + nc] for j in range(r)], axis=2)
    blocks = blocks + pe[None, None, :, None, :]
    flat = jnp.moveaxis(blocks, 3, 2).reshape(B, nc, G, CMP_BLOCK * hd)
    return jax.nn.silu(flat @ w1) @ w2


def to_sel_blocks(k, ns):
    B, L, G, hd = k.shape
    k = jnp.pad(k, ((0, 0), (0, ns * SEL_BLOCK - L), (0, 0), (0, 0)))
    return jnp.transpose(k.reshape(B, ns, SEL_BLOCK, G, hd), (0, 3, 1, 2, 4))


def selection_importance(imp_c, ns):
    rho = SEL_BLOCK // CMP_STRIDE
    r = CMP_BLOCK // CMP_STRIDE
    nc = imp_c.shape[-1]
    tail = rho * (ns + 1) - (r - 1) - nc
    padded = jnp.pad(imp_c, [(0, 0)] * (imp_c.ndim - 1) + [(r - 1, tail)])
    P = padded.reshape(imp_c.shape[:-1] + (ns + 1, rho))
    return P[..., :ns, :].sum(-1) + P[..., 1:, :r - 1].sum(-1)


def nsa_core(q, q_rot, q_pos, kc, vc, cmp_end, ks_blk, vs_blk, kw, vw, w_pos, gates):
    B, Q, H, hd = q.shape
    G = kc.shape[2]
    R = H // G
    NS = ks_blk.shape[2]
    scale = hd ** -0.5
    qg = q.reshape(B, Q, G, R, hd)
    qr = q_rot.reshape(B, Q, G, R, hd)
    s_c = jnp.einsum('bqgrd,bcgd->bqgrc', qg, kc).astype(F32) * scale
    m_c = (cmp_end[None, :] <= q_pos[:, None])[None, :, None, None, :]
    p_c = masked_softmax(s_c, m_c)
    o_c = jnp.einsum('bqgrc,bcgd->bqgrd', p_c.astype(vc.dtype), vc)
    imp = selection_importance(p_c.sum(3), NS)
    blk = jnp.arange(NS)
    valid = ((blk * SEL_BLOCK)[None, :] <= q_pos[:, None])[None, :, None, :]
    forced = ((blk[None, :] == 0) | (blk[None, :] == (q_pos // SEL_BLOCK)[:, None]))[None, :, None, :]
    score = jnp.where(forced, jnp.inf, jnp.where(valid, imp, -jnp.inf))
    k_eff = min(N_SEL, NS)
    _, idx = lax.top_k(score, k_eff)
    idx_t = jnp.transpose(idx, (0, 2, 1, 3)).reshape(B, G, Q * k_eff)
    bi = jnp.arange(B)[:, None, None]
    gi = jnp.arange(G)[None, :, None]
    k_sel = ks_blk[bi, gi, idx_t].reshape(B, G, Q, k_eff * SEL_BLOCK, hd)
    v_sel = vs_blk[bi, gi, idx_t].reshape(B, G, Q, k_eff * SEL_BLOCK, hd)
    tok = (idx[..., None] * SEL_BLOCK + jnp.arange(SEL_BLOCK)).reshape(B, Q, G, k_eff * SEL_BLOCK)
    m_s = (tok <= q_pos[None, :, None, None])[:, :, :, None, :]
    s_s = jnp.einsum('bqgrd,bgqkd->bqgrk', qr, k_sel).astype(F32) * scale
    p_s = masked_softmax(s_s, m_s)
    o_s = jnp.einsum('bqgrk,bgqkd->bqgrd', p_s.astype(v_sel.dtype), v_sel)
    m_w = ((w_pos[None, :] <= q_pos[:, None]) & (w_pos[None, :] > q_pos[:, None] - WINDOW) & (w_pos[None, :] >= 0))[None, :, None, None, :]
    s_w = jnp.einsum('bqgrd,bkgd->bqgrk', qr, kw).astype(F32) * scale
    p_w = masked_softmax(s_w, m_w)
    o_w = jnp.einsum('bqgrk,bkgd->bqgrd', p_w.astype(vw.dtype), vw)
    g = gates.reshape(B, Q, G, R, 3)
    o = g[..., 0:1] * o_c + g[..., 1:2] * o_s + g[..., 2:3] * o_w
    return o.reshape(B, Q, H * hd).astype(q.dtype)


def merge(o_a, o_b, ma, mb, w_a, w_b, w_o):
    h = jax.nn.sigmoid(ma) * (o_a @ w_a) + jax.nn.sigmoid(mb) * (o_b @ w_b)
    return h @ w_o


def mixer_prompt(x, w_in, lb, gnorm, cmp_pe, cmp_w1, cmp_w2, w_a, w_b, w_o):
    B, L, _ = x.shape
    qa, fa, ia, ga, qb, kvb, gb, ma, mb = project(x, w_in)
    q, k, v, log_f = hgrn_features(qa, fa, ia, lb)
    o_raw, S = hgrn_scan(q, k, v, log_f, jnp.zeros((B, A_HEADS, A_DK, A_DV), F32))
    o_a = hgrn_output(o_raw, ga, gnorm)
    pos = jnp.arange(L)
    qh, qr, kv4, kvw, gates = nsa_features(qb, kvb, gb, pos)
    k_c, v_c, k_s, v_s = kv4
    k_w, v_w = kvw
    kc = compress(k_c, cmp_pe[0], cmp_w1[0], cmp_w2[0])
    vc = compress(v_c, cmp_pe[1], cmp_w1[1], cmp_w2[1])
    cmp_end = jnp.arange(kc.shape[1]) * CMP_STRIDE + CMP_BLOCK - 1
    ns = -(-L // SEL_BLOCK)
    ks_blk = to_sel_blocks(k_s, ns)
    vs_blk = to_sel_blocks(v_s, ns)
    kw_pad = jnp.pad(k_w, ((0, 0), (WINDOW, 0), (0, 0), (0, 0)))
    vw_pad = jnp.pad(v_w, ((0, 0), (WINDOW, 0), (0, 0), (0, 0)))

    def query_block(n):
        s0 = n * Q_BLOCK
        sl = lambda a: lax.dynamic_slice_in_dim(a, s0, Q_BLOCK, axis=1)
        band = lambda a: lax.dynamic_slice_in_dim(a, s0, WINDOW + Q_BLOCK, axis=1)
        q_pos = s0 + jnp.arange(Q_BLOCK)
        w_pos = s0 - WINDOW + jnp.arange(WINDOW + Q_BLOCK)
        return nsa_core(sl(qh), sl(qr), q_pos, kc, vc, cmp_end, ks_blk, vs_blk, band(kw_pad), band(vw_pad), w_pos, sl(gates))

    o_b = lax.map(query_block, jnp.arange(L // Q_BLOCK))
    o_b = jnp.moveaxis(o_b, 0, 1).reshape(B, L, NSA_WIDTH_Q)
    y = merge(o_a, o_b, ma, mb, w_a, w_b, w_o)
    rows = jnp.stack([k_c, v_c, k_s, v_s], axis=2)
    wb = min(WINDOW, L)
    win = jnp.stack([k_w, v_w], axis=2)[:, L - wb:]
    return y, rows, win, S


def mixer_sample(x, cache_kv, win_buf, s0, page_table, w_in, lb, gnorm, cmp_pe, cmp_w1, cmp_w2, w_a, w_b, w_o):
    B, L, _ = x.shape
    qa, fa, ia, ga, qb, kvb, gb, ma, mb = project(x, w_in)
    q, k, v, log_f = hgrn_features(qa, fa, ia, lb)
    o_raw, S = hgrn_scan(q, k, v, log_f, s0.astype(F32))
    o_a = hgrn_output(o_raw, ga, gnorm)
    pos = PAST_LEN + jnp.arange(L)
    qh, qr, kv4, kvw, gates = nsa_features(qb, kvb, gb, pos)
    k_w, v_w = kvw
    rows = jnp.stack(kv4, axis=2)
    n_pages = page_table.shape[1]
    past = cache_kv[page_table].reshape(B, n_pages * PAGE_SIZE, 4, NSA_KV, NSA_HD)
    full = jnp.concatenate([past, rows.astype(past.dtype)], axis=1)
    lt = full.shape[1]
    kc = compress(full[:, :, 0], cmp_pe[0], cmp_w1[0], cmp_w2[0])
    vc = compress(full[:, :, 1], cmp_pe[1], cmp_w1[1], cmp_w2[1])
    cmp_end = jnp.arange(kc.shape[1]) * CMP_STRIDE + CMP_BLOCK - 1
    ns = -(-lt // SEL_BLOCK)
    ks_blk = to_sel_blocks(full[:, :, 2], ns)
    vs_blk = to_sel_blocks(full[:, :, 3], ns)
    wb = win_buf.shape[1]
    win_all = jnp.concatenate([win_buf, jnp.stack([k_w, v_w], axis=2).astype(win_buf.dtype)], axis=1)
    w_pos = PAST_LEN - wb + jnp.arange(wb + L)
    o_b = nsa_core(qh, qr, pos, kc, vc, cmp_end, ks_blk, vs_blk, win_all[:, :, 0], win_all[:, :, 1], w_pos, gates)
    y = merge(o_a, o_b, ma, mb, w_a, w_b, w_o)
    return y, rows, win_all[:, L:], S


def moe(x, w_router, b_router, w1, w3, w2):
    B, L, D = x.shape
    xt = x.reshape(B * L, D)
    probs = jax.nn.softmax((xt @ w_router).astype(F32) + b_router.astype(F32), axis=-1)
    grp = probs.reshape(-1, N_GROUPS, EXPERTS_PER_GROUP)
    grp_score = lax.top_k(grp, TOP_K)[0].sum(-1)
    g_sel = jnp.argmax(grp_score, axis=-1)
    in_grp = (jnp.arange(N_EXPERTS) // EXPERTS_PER_GROUP)[None, :] == g_sel[:, None]
    top_v, top_i = lax.top_k(jnp.where(in_grp, probs, -1.0), TOP_K)
    wts = top_v / top_v.sum(-1, keepdims=True)
    comb = jnp.einsum('tk,tke->te', wts, jax.nn.one_hot(top_i, N_EXPERTS, dtype=F32)).astype(x.dtype)
    y = jnp.zeros_like(xt)
    for e in range(N_EXPERTS):
        h = jax.nn.silu(xt @ w1[e]) * (xt @ w3[e])
        y = y + comb[:, e:e + 1] * (h @ w2[e])
    return y.reshape(B, L, D)


def setup_inputs(seed: int = 0) -> dict:
    key = jax.random.key(seed)
    ks = jax.random.split(key, 24)
    n_pages = PAST_LEN // PAGE_SIZE
    n_used = DEC_BATCH * n_pages
    n_pool = n_used + max(1, n_used // 4)
    wb = min(WINDOW, PAST_LEN)

    def nrm(k, shape, s=1.0):
        return jax.random.normal(k, shape, F32) * s

    page_table = jax.random.permutation(ks[5], n_pool)[:n_used].reshape(DEC_BATCH, n_pages).astype(jnp.int32)
    return {
        'x_prompt': nrm(ks[0], (BATCH, SEQ, D_MODEL)),
        'x_sample': nrm(ks[1], (DEC_BATCH, DEC_SEQ, D_MODEL)),
        'cache_nsa_kv': nrm(ks[2], (DEPTH, n_pool, PAGE_SIZE, 4, NSA_KV, NSA_HD)),
        'cache_nsa_win': nrm(ks[3], (DEPTH, DEC_BATCH, wb, 2, NSA_KV, NSA_HD)),
        'state_hgrn': nrm(ks[4], (DEPTH, DEC_BATCH, A_HEADS, A_DK, A_DV), 0.5),
        'page_table': page_table,
        'w_in': nrm(ks[6], (DEPTH, D_MODEL, D_IN), D_MODEL ** -0.5),
        'lb_raw': nrm(ks[7], (DEPTH, A_WIDTH_K), 0.5),
        'gnorm_a': 1.0 + nrm(ks[8], (DEPTH, A_WIDTH_V), 0.1),
        'cmp_pe': nrm(ks[9], (DEPTH, 2, CMP_BLOCK, NSA_HD), 0.1),
        'cmp_w1': nrm(ks[10], (DEPTH, 2, CMP_BLOCK * NSA_HD, CMP_HIDDEN), (CMP_BLOCK * NSA_HD) ** -0.5),
        'cmp_w2': nrm(ks[11], (DEPTH, 2, CMP_HIDDEN, NSA_HD), CMP_HIDDEN ** -0.5),
        'w_branch_a': nrm(ks[12], (DEPTH, A_WIDTH_V, D_MODEL), BETA * A_WIDTH_V ** -0.5),
        'w_branch_b': nrm(ks[13], (DEPTH, NSA_WIDTH_Q, D_MODEL), BETA * NSA_WIDTH_Q ** -0.5),
        'w_out': nrm(ks[14], (DEPTH, D_MODEL, D_MODEL), BETA * D_MODEL ** -0.5),
        'ln1_g': 1.0 + nrm(ks[15], (DEPTH, D_MODEL), 0.1),
        'ln1_b': nrm(ks[16], (DEPTH, D_MODEL), 0.02),
        'ln2_g': 1.0 + nrm(ks[17], (DEPTH, D_MODEL), 0.1),
        'ln2_b': nrm(ks[18], (DEPTH, D_MODEL), 0.02),
        'w_router': nrm(ks[19], (D_MODEL, N_EXPERTS), D_MODEL ** -0.5),
        'b_router': nrm(ks[20], (N_EXPERTS,), 0.01),
        'w_e1': nrm(ks[21], (DEPTH, N_EXPERTS, D_MODEL, D_FF_EXPERT), D_MODEL ** -0.5),
        'w_e3': nrm(ks[22], (DEPTH, N_EXPERTS, D_MODEL, D_FF_EXPERT), D_MODEL ** -0.5),
        'w_e2': nrm(ks[23], (DEPTH, N_EXPERTS, D_FF_EXPERT, D_MODEL), BETA * D_FF_EXPERT ** -0.5),
    }


def reference(x_prompt, x_sample, cache_nsa_kv, cache_nsa_win, state_hgrn, page_table, w_in, lb_raw, gnorm_a, cmp_pe, cmp_w1, cmp_w2, w_branch_a, w_branch_b, w_out, ln1_g, ln1_b, ln2_g, ln2_b, w_router, b_router, w_e1, w_e3, w_e2):
    lbs = jnp.cumsum(jax.nn.softmax(lb_raw.astype(F32), axis=0), axis=0)
    lbs = lbs - lbs[0:1]
    xp, xs = x_prompt, x_sample
    rows_p, rows_s, win_p, win_s, st_p, st_s = [], [], [], [], [], []
    for l in range(DEPTH):
        lw = (w_in[l], lbs[l], gnorm_a[l], cmp_pe[l], cmp_w1[l], cmp_w2[l], w_branch_a[l], w_branch_b[l], w_out[l])
        mp, r_p, wp, sp = mixer_prompt(xp, *lw)
        xp = layer_norm(ALPHA * xp + mp, ln1_g[l], ln1_b[l])
        xp = layer_norm(ALPHA * xp + moe(xp, w_router, b_router, w_e1[l], w_e3[l], w_e2[l]), ln2_g[l], ln2_b[l])
        ms, r_s, ws, ss = mixer_sample(xs, cache_nsa_kv[l], cache_nsa_win[l], state_hgrn[l], page_table, *lw)
        xs = layer_norm(ALPHA * xs + ms, ln1_g[l], ln1_b[l])
        xs = layer_norm(ALPHA * xs + moe(xs, w_router, b_router, w_e1[l], w_e3[l], w_e2[l]), ln2_g[l], ln2_b[l])
        rows_p.append(r_p); rows_s.append(r_s)
        win_p.append(wp); win_s.append(ws)
        st_p.append(sp); st_s.append(ss)
    return (xp, xs, jnp.stack(rows_p), jnp.stack(rows_s), jnp.stack(win_p), jnp.stack(win_s), jnp.stack(st_p), jnp.stack(st_s))
```

```python
import functools
import math

import jax
import jax.numpy as jnp
import numpy as np
from jax import lax
from jax.experimental import pallas as pl
from jax.experimental.pallas import tpu as pltpu

F32 = jnp.float32
BF16 = jnp.bfloat16

A_HEADS = 8
A_DK = 128
NSA_HEADS = 16
NSA_KV = 4
NSA_REP = NSA_HEADS // NSA_KV
NSA_HD = 64
CMP_BLOCK = 32
CMP_STRIDE = 16
CMP_HIDDEN = 128
SEL_BLOCK = 64
N_SEL = 16
WINDOW = 512
PAGE_SIZE = 128
ROPE_THETA = 10000.0
N_EXPERTS = 16
N_GROUPS = 4
EXPERTS_PER_GROUP = N_EXPERTS // N_GROUPS
LN_EPS = 1e-5
NEG = -1e30

VMEM_LIMIT_BYTES = 56 * 1024 * 1024
LANES = 128


def _cparams(*sem):
    return pltpu.CompilerParams(dimension_semantics=sem, vmem_limit_bytes=VMEM_LIMIT_BYTES)


def _sigmoid(x):
    return 1.0 / (1.0 + jnp.exp(-x))


def _silu(x):
    return x * _sigmoid(x)


def _mm_kernel(x_ref, w_ref, o_ref, wb_ref):
    @pl.when(pl.program_id(1) == 0)
    def _():
        wb_ref[...] = w_ref[...].astype(BF16)

    o_ref[...] = jnp.dot(x_ref[...], wb_ref[...], preferred_element_type=F32).astype(o_ref.dtype)


def matmul(x, w, layer, *, tm, tn, col0=0, ncols=None, out_dtype=F32):
    M, K = x.shape
    ncols = w.shape[-1] if ncols is None else ncols
    assert M % tm == 0 and ncols % tn == 0 and col0 % tn == 0
    off = col0 // tn
    return pl.pallas_call(
        _mm_kernel,
        grid=(ncols // tn, M // tm),
        in_specs=[pl.BlockSpec((tm, K), lambda j, i: (i, 0)),
                  pl.BlockSpec((None, K, tn), lambda j, i: (layer, 0, j + off))],
        out_specs=pl.BlockSpec((tm, tn), lambda j, i: (i, j)),
        out_shape=jax.ShapeDtypeStruct((M, ncols), out_dtype),
        scratch_shapes=[pltpu.VMEM((K, tn), BF16)],
        compiler_params=_cparams("arbitrary", "arbitrary"),
        name="matmul",
    )(x, w)


def _ln_kernel(alpha, x_ref, y_ref, g_ref, b_ref, o_ref, ob_ref):
    v = alpha * x_ref[...] + y_ref[...]
    mu = jnp.mean(v, axis=-1, keepdims=True)
    d = v - mu
    var = jnp.mean(d * d, axis=-1, keepdims=True)
    out = d * lax.rsqrt(var + LN_EPS) * g_ref[...] + b_ref[...]
    o_ref[...] = out
    ob_ref[...] = out.astype(BF16)


def residual_layer_norm(x, y, g, b, alpha, *, tm=256):
    M, D = x.shape
    spec = pl.BlockSpec((tm, D), lambda i: (i, 0))
    vec = pl.BlockSpec((1, D), lambda i: (0, 0))
    return pl.pallas_call(
        functools.partial(_ln_kernel, alpha),
        grid=(M // tm,),
        in_specs=[spec, spec, vec, vec],
        out_specs=[spec, spec],
        out_shape=[jax.ShapeDtypeStruct((M, D), F32), jax.ShapeDtypeStruct((M, D), BF16)],
        compiler_params=_cparams("parallel"),
        name="residual_ln",
    )(x, y, g.reshape(1, D), b.reshape(1, D))


def _rope_kernel(scale, x_ref, cos_ref, sin_ref, o_ref):
    cos = cos_ref[...]
    sin = sin_ref[...]
    lane = lax.broadcasted_iota(jnp.int32, cos.shape, 1)
    first_half = (lane % NSA_HD) < (NSA_HD // 2)
    for j in range(x_ref.shape[1] // LANES):
        x = x_ref[:, j * LANES:(j + 1) * LANES]
        partner = jnp.where(first_half, pltpu.roll(x, LANES - NSA_HD // 2, 1), pltpu.roll(x, NSA_HD // 2, 1))
        o_ref[:, j * LANES:(j + 1) * LANES] = ((x * cos + partner * sin) * scale).astype(o_ref.dtype)


def rope(z, col0, width, cos, sin, *, scale=1.0, out_dtype=F32, tm=256):
    M = z.shape[0]
    assert col0 % width == 0 and width % LANES == 0
    return pl.pallas_call(
        functools.partial(_rope_kernel, scale),
        grid=(M // tm,),
        in_specs=[pl.BlockSpec((tm, width), lambda i: (i, col0 // width)),
                  pl.BlockSpec((tm, LANES), lambda i: (i, 0)),
                  pl.BlockSpec((tm, LANES), lambda i: (i, 0))],
        out_specs=pl.BlockSpec((tm, width), lambda i: (i, 0)),
        out_shape=jax.ShapeDtypeStruct((M, width), out_dtype),
        compiler_params=_cparams("parallel"),
        name="rope",
    )(z, cos, sin)


def rope_tables(pos):
    half = NSA_HD // 2
    inv = jnp.power(ROPE_THETA, -jnp.arange(half, dtype=F32) / half)
    ang = pos.astype(F32)[:, None] * inv[None, :]
    cos, sin = jnp.cos(ang), jnp.sin(ang)
    reps = LANES // NSA_HD
    return jnp.tile(jnp.concatenate([cos, cos], -1), (1, reps)), jnp.tile(jnp.concatenate([-sin, sin], -1), (1, reps))


HG_CHUNK = 64
HG_SUB = 16
HG_BLOCK = 256


def _split3(x):
    hi = x.astype(BF16)
    r1 = x - hi.astype(F32)
    mid = r1.astype(BF16)
    lo = (r1 - mid.astype(F32)).astype(BF16)
    return hi, mid, lo


def _log_forget(fa, loglb, log1mlb):
    log_sig = jnp.minimum(fa, 0.0) - jnp.log(1.0 + jnp.exp(-jnp.abs(fa)))
    b = log1mlb + log_sig
    mx = jnp.maximum(loglb, b)
    return mx + jnp.log(1.0 + jnp.exp(-jnp.abs(loglb - b)))


def _hgrn_gate_norm(o, ga, gn):
    o = o * lax.rsqrt(jnp.mean(o * o, axis=-1, keepdims=True) + 1e-6)
    return o * gn * _silu(ga)


def _hgrn_prompt_kernel(qa_ref, fa_ref, ia_ref, ga_ref, loglb_ref, log1mlb_ref, onemlb_ref, gn_ref,
                        o_ref, s_ref, st_ref):
    c = pl.program_id(1)

    @pl.when(c == 0)
    def _():
        st_ref[...] = jnp.zeros_like(st_ref)

    loglb, log1mlb, onemlb, gn = loglb_ref[...], log1mlb_ref[...], onemlb_ref[...], gn_ref[...]
    row = lax.broadcasted_iota(jnp.int32, (HG_CHUNK, HG_CHUNK), 0)
    col = lax.broadcasted_iota(jnp.int32, (HG_CHUNK, HG_CHUNK), 1)
    tril = (col <= row).astype(BF16)
    srow = lax.broadcasted_iota(jnp.int32, (HG_SUB, HG_SUB), 0)
    scol = lax.broadcasted_iota(jnp.int32, (HG_SUB, HG_SUB), 1)

    def chunk(ci, carry):
        r0 = pl.multiple_of(ci * HG_CHUNK, HG_CHUNK)
        rows = pl.ds(r0, HG_CHUNK)
        qa, fa, v = qa_ref[rows, :], fa_ref[rows, :], ia_ref[rows, :]
        q = _silu(qa)
        logf = _log_forget(fa, loglb, log1mlb)
        k = onemlb * _sigmoid(-fa)
        b = sum(jnp.dot(tril, t, preferred_element_type=F32) for t in _split3(logf))
        st = st_ref[...]
        vb = v.astype(BF16)
        o_inter = lax.dot_general((q * jnp.exp(b)).astype(BF16), st.astype(BF16), (((1,), (1,)), ((), ())),
                                  preferred_element_type=F32)
        for i in range(HG_CHUNK // HG_SUB):
            lo = i * HG_SUB
            qi, bi, ki = q[lo:lo + HG_SUB], b[lo:lo + HG_SUB], k[lo:lo + HG_SUB]
            o = o_inter[lo:lo + HG_SUB]
            if i > 0:
                anchor = b[lo - 1:lo]
                qt = (qi * jnp.exp(bi - anchor)).astype(BF16)
                kt = (k[:lo] * jnp.exp(anchor - b[:lo])).astype(BF16)
                a_off = lax.dot_general(qt, kt, (((1,), (1,)), ((), ())), preferred_element_type=F32)
                o = o + jnp.dot(a_off.astype(BF16), vb[:lo], preferred_element_type=F32)
            a_d = jnp.zeros((HG_SUB, HG_SUB), F32)
            for s in range(HG_SUB):
                p = qi * ki[s:s + 1] * jnp.exp(jnp.minimum(bi - bi[s:s + 1], 0.0))
                a_d = jnp.where(scol == s, jnp.sum(p, axis=-1, keepdims=True), a_d)
            a_d = jnp.where(scol <= srow, a_d, 0.0)
            o = o + jnp.dot(a_d.astype(BF16), vb[lo:lo + HG_SUB], preferred_element_type=F32)
            ga = ga_ref[pl.ds(r0 + lo, HG_SUB), :]
            o_ref[pl.ds(r0 + lo, HG_SUB), :] = _hgrn_gate_norm(o, ga, gn).astype(o_ref.dtype)
        b_last = b[HG_CHUNK - 1:HG_CHUNK]
        kt = (k * jnp.exp(b_last - b)).astype(BF16)
        st_ref[...] = st * jnp.exp(b_last) + lax.dot_general(vb, kt, (((0,), (0,)), ((), ())),
                                                             preferred_element_type=F32)
        return carry

    lax.fori_loop(0, HG_BLOCK // HG_CHUNK, chunk, 0)

    @pl.when(c == pl.num_programs(1) - 1)
    def _():
        s_ref[...] = st_ref[...].T


def hgrn_prompt(z, L, loglb, log1mlb, onemlb, gnorm):
    blk = lambda part: pl.BlockSpec((HG_BLOCK, A_DK), lambda h, c: (c, part * A_HEADS + h))
    vec = pl.BlockSpec((1, A_DK), lambda h, c: (0, h))
    return pl.pallas_call(
        _hgrn_prompt_kernel,
        grid=(A_HEADS, L // HG_BLOCK),
        in_specs=[blk(0), blk(1), blk(2), blk(3), vec, vec, vec, vec],
        out_specs=[pl.BlockSpec((HG_BLOCK, A_DK), lambda h, c: (c, h)),
                   pl.BlockSpec((None, A_DK, A_DK), lambda h, c: (h, 0, 0))],
        out_shape=[jax.ShapeDtypeStruct((L, A_HEADS * A_DK), BF16),
                   jax.ShapeDtypeStruct((A_HEADS, A_DK, A_DK), F32)],
        scratch_shapes=[pltpu.VMEM((A_DK, A_DK), F32)],
        compiler_params=_cparams("arbitrary", "arbitrary"),
        name="hgrn_prompt",
    )(z, z, z, z, loglb, log1mlb, onemlb, gnorm)


CMP_PAGES = 16
CMP_ROWS = CMP_PAGES * PAGE_SIZE
CMP_OUT = CMP_ROWS // CMP_STRIDE
CMP_W = NSA_KV * NSA_HD
CMP_PLANES = 2 * CMP_W // LANES
GROUPS_PER_PLANE = LANES // NSA_HD


def _compress_copies(pt_ref, cache_ref, xbuf, sem, layer, bb, cc, slot, n_pages):
    out = []
    nxt = jnp.minimum((cc + 1) * CMP_PAGES, n_pages - 1)
    for q in range(CMP_PLANES):
        cols = pl.ds(q * LANES, LANES)
        for j in range(CMP_PAGES):
            pg = pt_ref[bb, cc * CMP_PAGES + j]
            out.append(pltpu.make_async_copy(cache_ref.at[layer, pg, :, cols],
                                             xbuf.at[slot, q, pl.ds(j * PAGE_SIZE, PAGE_SIZE), :], sem.at[slot]))
        out.append(pltpu.make_async_copy(cache_ref.at[layer, pt_ref[bb, nxt], pl.ds(0, CMP_STRIDE), cols],
                                         xbuf.at[slot, q, pl.ds(CMP_ROWS, CMP_STRIDE), :], sem.at[slot]))
    return out


def _compress_kernel(layer, pt_ref, cache_ref, w1_hbm, pe_ref, w2_ref, kc_ref, vc_ref, xbuf, w1buf, sem, wsem):
    b, c = pl.program_id(0), pl.program_id(1)
    nb, nc = pl.num_programs(0), pl.num_programs(1)
    step = b * nc + c
    slot = step % 2
    n_pages = nc * CMP_PAGES
    mk = functools.partial(_compress_copies, pt_ref, cache_ref, xbuf, sem, layer)

    @pl.when(step == 0)
    def _():
        wcopy = pltpu.make_async_copy(w1_hbm, w1buf, wsem)
        wcopy.start()
        for cp in mk(b, c, slot, n_pages):
            cp.start()
        wcopy.wait()

    @pl.when(step + 1 < nb * nc)
    def _():
        last_c = c == nc - 1
        for cp in mk(jnp.where(last_c, b + 1, b), jnp.where(last_c, 0, c + 1), 1 - slot, n_pages):
            cp.start()

    for cp in mk(b, c, slot, n_pages):
        cp.wait()

    planes_per_out = CMP_PLANES // 2
    for t, o_ref in enumerate((kc_ref, vc_ref)):
        for q in range(planes_per_out):
            xs = xbuf.at[slot, t * planes_per_out + q]
            acc = jnp.zeros((CMP_OUT, GROUPS_PER_PLANE * CMP_HIDDEN), F32)
            for p in range(CMP_BLOCK):
                x = xs[pl.ds(p, CMP_OUT, stride=CMP_STRIDE), :] + pe_ref[t, p]
                acc = acc + jnp.dot(x.astype(BF16), w1buf[t, p], preferred_element_type=F32)
            o_ref[:, q * LANES:(q + 1) * LANES] = jnp.dot(_silu(acc).astype(BF16), w2_ref[t],
                                                          preferred_element_type=F32)


def compress_weights(pe, w1, w2):
    eye = jnp.eye(GROUPS_PER_PLANE, dtype=F32)
    w1r = w1.reshape(2, CMP_BLOCK, NSA_HD, CMP_HIDDEN)
    w1bd = jnp.einsum('gh,tpdn->tpgdhn', eye, w1r).reshape(2, CMP_BLOCK, LANES, GROUPS_PER_PLANE * CMP_HIDDEN)
    w2bd = jnp.einsum('gh,tnd->tgnhd', eye, w2).reshape(2, GROUPS_PER_PLANE * CMP_HIDDEN, LANES)
    pet = jnp.tile(pe, (1, 1, GROUPS_PER_PLANE)).reshape(2, CMP_BLOCK, 1, LANES)
    return w1bd.astype(BF16), pet, w2bd.astype(BF16)


def compress_paged(cache, layer, page_table, w1bd, pet, w2bd):
    B, n_pages = page_table.shape
    assert n_pages % CMP_PAGES == 0
    nc = n_pages // CMP_PAGES
    out = jax.ShapeDtypeStruct((B, nc * CMP_OUT, CMP_W), F32)
    ospec = pl.BlockSpec((None, CMP_OUT, CMP_W), lambda b, c, pt: (b, c, 0))
    return pl.pallas_call(
        functools.partial(_compress_kernel, layer),
        grid_spec=pltpu.PrefetchScalarGridSpec(
            num_scalar_prefetch=1,
            grid=(B, nc),
            in_specs=[pl.BlockSpec(memory_space=pl.ANY),
                      pl.BlockSpec(memory_space=pl.ANY),
                      pl.BlockSpec(pet.shape, lambda b, c, pt: (0, 0, 0, 0)),
                      pl.BlockSpec(w2bd.shape, lambda b, c, pt: (0, 0, 0))],
            out_specs=[ospec, ospec],
            scratch_shapes=[pltpu.VMEM((2, CMP_PLANES, CMP_ROWS + CMP_STRIDE, LANES), F32),
                            pltpu.VMEM(w1bd.shape, BF16),
                            pltpu.SemaphoreType.DMA((2,)),
                            pltpu.SemaphoreType.DMA(())]),
        out_shape=[out, out],
        compiler_params=_cparams("arbitrary", "arbitrary"),
        name="nsa_compress",
    )(page_table, cache, w1bd, pet, w2bd)


NSA_TQ = 256
NSA_TK = 512
SEL_PER_CMP = SEL_BLOCK // CMP_STRIDE
NT_DIMS = (((1,), (1,)), ((), ()))


def _masked_softmax(s, mask):
    s = jnp.where(mask, s, NEG)
    e = jnp.exp(s - jnp.max(s, axis=-1, keepdims=True)) * mask.astype(F32)
    d = jnp.sum(e, axis=-1, keepdims=True)
    return e / jnp.where(d > 0, d, 1.0)


def _pool_matrix(n_cmp, n_sel):
    i = lax.broadcasted_iota(jnp.int32, (n_cmp, n_sel), 0)
    j = lax.broadcasted_iota(jnp.int32, (n_cmp, n_sel), 1)
    lo = SEL_PER_CMP * j - (CMP_BLOCK // CMP_STRIDE - 1)
    return ((i >= lo) & (i <= SEL_PER_CMP * j + SEL_PER_CMP - 1)).astype(BF16)


def _importance(psum, pool):
    return sum(jnp.dot(t, pool, preferred_element_type=F32) for t in _split3(psum))


def _top_k_mask(score, k):
    lane = lax.broadcasted_iota(jnp.int32, score.shape, 1).astype(F32)
    sel = jnp.zeros(score.shape, F32)
    for _ in range(k):
        m = jnp.max(score, axis=-1, keepdims=True)
        idx = jnp.min(jnp.where(score == m, lane, float(score.shape[-1])), axis=-1, keepdims=True)
        hit = lane == idx
        sel = jnp.where(hit, 1.0, sel)
        score = jnp.where(hit, -jnp.inf, score)
    return sel


def _nsa_prompt_kernel(q_ref, qr_ref, kc_ref, vc_ref, ks_ref, vs_ref, kw_ref, vw_ref, gate_ref, o_ref,
                       sel_ref, m_ref, l_ref, acc_ref, oc_ref):
    tq, tk = NSA_TQ, NSA_TK
    s0 = pl.program_id(1) * tq
    tpos = s0 + lax.broadcasted_iota(jnp.int32, (tq, 1), 0)
    n_cmp = kc_ref.shape[0]
    n_sel = sel_ref.shape[1]

    cidx = lax.broadcasted_iota(jnp.int32, (tq, n_cmp), 1)
    cmask = (cidx * CMP_STRIDE + (CMP_BLOCK - 1) <= tpos) & (cidx < n_cmp - 1)
    psum = jnp.zeros((tq, n_cmp), F32)
    for r in range(NSA_REP):
        s = lax.dot_general(q_ref[r], kc_ref[...], NT_DIMS, preferred_element_type=F32)
        p = _masked_softmax(s, cmask)
        oc_ref[r] = jnp.dot(p.astype(BF16), vc_ref[...], preferred_element_type=F32)
        psum = psum + p

    imp = _importance(psum, _pool_matrix(n_cmp, n_sel))
    blk = lax.broadcasted_iota(jnp.int32, (tq, n_sel), 1)
    valid = blk * SEL_BLOCK <= tpos
    forced = (blk == 0) | (blk == tpos // SEL_BLOCK)
    score = jnp.where(forced, jnp.inf, jnp.where(valid, imp, -jnp.inf))
    sel_ref[...] = _top_k_mask(score, N_SEL).astype(BF16)

    m_ref[...] = jnp.full(m_ref.shape, NEG, F32)
    l_ref[...] = jnp.zeros(l_ref.shape, F32)
    acc_ref[...] = jnp.zeros(acc_ref.shape, F32)

    def sweep(kt, carry):
        k0 = pl.multiple_of(kt * tk, tk)
        jj = lax.broadcasted_iota(jnp.int32, (n_sel, tk), 0)
        ss = lax.broadcasted_iota(jnp.int32, (n_sel, tk), 1)
        expand = (jj == kt * (tk // SEL_BLOCK) + ss // SEL_BLOCK).astype(BF16)
        chosen = jnp.dot(sel_ref[...], expand, preferred_element_type=F32)
        kpos = k0 + lax.broadcasted_iota(jnp.int32, (tq, tk), 1)
        allowed = (chosen > 0.5) & (kpos <= tpos)
        allowed_f = allowed.astype(F32)
        ks = ks_ref[pl.ds(k0, tk), :]
        vs = vs_ref[pl.ds(k0, tk), :]
        for r in range(NSA_REP):
            s = lax.dot_general(qr_ref[r], ks, NT_DIMS, preferred_element_type=F32)
            s = jnp.where(allowed, s, NEG)
            m_old = m_ref[r]
            m_new = jnp.maximum(m_old, jnp.max(s, axis=-1, keepdims=True))
            p = jnp.exp(s - m_new) * allowed_f
            alpha = jnp.exp(m_old - m_new)
            l_ref[r] = alpha * l_ref[r] + jnp.sum(p, axis=-1, keepdims=True)
            acc_ref[r] = alpha * acc_ref[r] + jnp.dot(p.astype(BF16), vs, preferred_element_type=F32)
            m_ref[r] = m_new
        return carry

    lax.fori_loop(0, (s0 + tq + tk - 1) // tk, sweep, 0)

    band = WINDOW + tq
    w0 = pl.multiple_of(jnp.maximum(s0 - WINDOW, 0), tq)
    wpos = w0 + lax.broadcasted_iota(jnp.int32, (tq, band), 1)
    wmask = (wpos <= tpos) & (wpos > tpos - WINDOW)
    kw = kw_ref[pl.ds(w0, band), :]
    vw = vw_ref[pl.ds(w0, band), :]
    gates = _sigmoid(gate_ref[...])
    for r in range(NSA_REP):
        s = lax.dot_general(qr_ref[r], kw, NT_DIMS, preferred_element_type=F32)
        o_w = jnp.dot(_masked_softmax(s, wmask).astype(BF16), vw, preferred_element_type=F32)
        o_s = acc_ref[r] / l_ref[r]
        o = (gates[:, 3 * r:3 * r + 1] * oc_ref[r] + gates[:, 3 * r + 1:3 * r + 2] * o_s
             + gates[:, 3 * r + 2:3 * r + 3] * o_w)
        o_ref[:, r * NSA_HD:(r + 1) * NSA_HD] = o.astype(o_ref.dtype)


def nsa_prompt(q, qr, kc, vc, ks, vs, kw, vw, gates):
    L = q.shape[1]
    n_cmp = kc.shape[1]
    n_sel = L // SEL_BLOCK
    assert L % NSA_TK == 0 and n_sel % LANES == 0 or n_sel <= LANES
    qspec = pl.BlockSpec((NSA_REP, NSA_TQ, NSA_HD), lambda g, i: (g, i, 0))
    cspec = pl.BlockSpec((None, n_cmp, NSA_HD), lambda g, i: (g, 0, 0))
    kspec = pl.BlockSpec((None, L, NSA_HD), lambda g, i: (g, 0, 0))
    return pl.pallas_call(
        _nsa_prompt_kernel,
        grid=(NSA_KV, L // NSA_TQ),
        in_specs=[qspec, qspec, cspec, cspec, kspec, kspec, kspec, kspec,
                  pl.BlockSpec((None, NSA_TQ, LANES), lambda g, i: (g, i, 0))],
        out_specs=pl.BlockSpec((NSA_TQ, NSA_REP * NSA_HD), lambda g, i: (i, g)),
        out_shape=jax.ShapeDtypeStruct((L, NSA_HEADS * NSA_HD), BF16),
        scratch_shapes=[pltpu.VMEM((NSA_TQ, n_sel), BF16),
                        pltpu.VMEM((NSA_REP, NSA_TQ, 1), F32),
                        pltpu.VMEM((NSA_REP, NSA_TQ, 1), F32),
                        pltpu.VMEM((NSA_REP, NSA_TQ, NSA_HD), F32),
                        pltpu.VMEM((NSA_REP, NSA_TQ, NSA_HD), F32)],
        compiler_params=_cparams("parallel", "arbitrary"),
        name="nsa_prompt",
    )(q, qr, kc, vc, ks, vs, kw, vw, gates)


def _top_k_indices(score, k):
    lane = lax.broadcasted_iota(jnp.int32, score.shape, 1).astype(F32)
    out = jnp.zeros(score.shape, F32)
    for t in range(k):
        m = jnp.max(score, axis=-1, keepdims=True)
        idx = jnp.min(jnp.where(score == m, lane, float(score.shape[-1])), axis=-1, keepdims=True)
        out = jnp.where(lane == float(t), idx, out)
        score = jnp.where(lane == idx, -jnp.inf, score)
    return out.astype(jnp.int32)


def _nsa_sample_select_kernel(q_pos, q_ref, kc_ref, vc_ref, oc_ref, idx_ref):
    n_cmp = kc_ref.shape[0]
    n_sel = idx_ref.shape[-1]
    q = q_ref[...]
    head_group = lax.broadcasted_iota(jnp.int32, (NSA_HEADS, 1), 0) // NSA_REP
    cidx = lax.broadcasted_iota(jnp.int32, (NSA_HEADS, n_cmp), 1)
    cmask = (cidx * CMP_STRIDE + (CMP_BLOCK - 1) <= q_pos) & (cidx < n_cmp - 1)
    grp_row = lax.broadcasted_iota(jnp.int32, (NSA_KV, 1), 0)
    oc = jnp.zeros((NSA_HEADS, NSA_HD), F32)
    psum = jnp.zeros((NSA_KV, n_cmp), F32)
    for g in range(NSA_KV):
        cols = slice(g * NSA_HD, (g + 1) * NSA_HD)
        s = lax.dot_general(q, kc_ref[:, cols].astype(BF16), NT_DIMS, preferred_element_type=F32)
        p = _masked_softmax(s, cmask)
        o = jnp.dot(p.astype(BF16), vc_ref[:, cols].astype(BF16), preferred_element_type=F32)
        mine = head_group == g
        oc = jnp.where(mine, o, oc)
        pg = jnp.sum(jnp.where(mine, p, 0.0), axis=0, keepdims=True)
        psum = jnp.where(grp_row == g, pg, psum)
    oc_ref[...] = oc
    imp = _importance(psum, _pool_matrix(n_cmp, n_sel))
    blk = lax.broadcasted_iota(jnp.int32, (NSA_KV, n_sel), 1)
    score = jnp.where(blk == 0, jnp.inf, imp)
    idx_ref[...] = _top_k_indices(score, N_SEL - 1)


def nsa_sample_select(q, kc, vc, q_pos, n_sel):
    B, n_cmp = kc.shape[0], kc.shape[1]
    cspec = pl.BlockSpec((None, n_cmp, CMP_W), lambda b: (b, 0, 0))
    hspec = pl.BlockSpec((None, NSA_HEADS, NSA_HD), lambda b: (b, 0, 0))
    return pl.pallas_call(
        functools.partial(_nsa_sample_select_kernel, q_pos),
        grid=(B,),
        in_specs=[hspec, cspec, cspec],
        out_specs=[hspec, pl.BlockSpec((None, NSA_KV, n_sel), lambda b: (b, 0, 0))],
        out_shape=[jax.ShapeDtypeStruct((B, NSA_HEADS, NSA_HD), F32),
                   jax.ShapeDtypeStruct((B, NSA_KV, n_sel), jnp.int32)],
        compiler_params=_cparams("parallel"),
        name="nsa_sample_select",
    )(q, kc, vc)


N_PAST_SEL = N_SEL - 1
SEL_K_COL = 2 * CMP_W
SEL_V_COL = 3 * CMP_W
BLOCKS_PER_PAGE = PAGE_SIZE // SEL_BLOCK


def _sel_copies(pt_ref, idx_ref, cache_ref, kbuf, vbuf, sem, layer, n_pages, b, slot):
    out = []
    for g in range(NSA_KV):
        for t in range(N_PAST_SEL):
            blk = idx_ref[(b * NSA_KV + g) * N_SEL + t]
            pg = pt_ref[b * n_pages + blk // BLOCKS_PER_PAGE]
            rows = pl.ds((blk % BLOCKS_PER_PAGE) * SEL_BLOCK, SEL_BLOCK)
            dst = pl.ds(t * SEL_BLOCK, SEL_BLOCK)
            plane = (g // GROUPS_PER_PLANE) * LANES
            out.append(pltpu.make_async_copy(cache_ref.at[layer, pg, rows, pl.ds(SEL_K_COL + plane, LANES)],
                                             kbuf.at[slot, g, dst, :], sem.at[slot]))
            out.append(pltpu.make_async_copy(cache_ref.at[layer, pg, rows, pl.ds(SEL_V_COL + plane, LANES)],
                                             vbuf.at[slot, g, dst, :], sem.at[slot]))
    return out


def _softmax_with_self(s, s_self, mask=None):
    if mask is not None:
        s = jnp.where(mask, s, NEG)
    m = jnp.maximum(jnp.max(s, axis=-1, keepdims=True), s_self)
    e = jnp.exp(s - m)
    if mask is not None:
        e = e * mask.astype(F32)
    e_self = jnp.exp(s_self - m)
    return e, e_self, jnp.sum(e, axis=-1, keepdims=True) + e_self


def _nsa_sample_attend_kernel(layer, n_pages, win_first, pt_ref, idx_ref, cache_ref, qr_ref, new_ref, win_ref, oc_ref,
                              gate_ref, o_ref, kbuf, vbuf, sem):
    b = pl.program_id(0)
    nb = pl.num_programs(0)
    slot = b % 2
    mk = functools.partial(_sel_copies, pt_ref, idx_ref, cache_ref, kbuf, vbuf, sem, layer, n_pages)

    @pl.when(b == 0)
    def _():
        for cp in mk(b, slot):
            cp.start()

    @pl.when(b + 1 < nb)
    def _():
        for cp in mk(b + 1, 1 - slot):
            cp.start()

    for cp in mk(b, slot):
        cp.wait()

    qr = qr_ref[...]
    qf = qr.astype(F32)
    head_group = lax.broadcasted_iota(jnp.int32, (NSA_HEADS, 1), 0) // NSA_REP
    gates = _sigmoid(gate_ref[...])
    n_win = win_ref.shape[0]
    wmask = lax.broadcasted_iota(jnp.int32, (NSA_HEADS, n_win), 1) >= win_first
    out = jnp.zeros((NSA_HEADS, NSA_HD), F32)
    for g in range(NSA_KV):
        cols = slice(g * NSA_HD, (g + 1) * NSA_HD)
        ks_new, vs_new = new_ref[0:1, cols], new_ref[1:2, cols]
        kw_new, vw_new = new_ref[2:3, cols], new_ref[3:4, cols]
        half = slice((g % GROUPS_PER_PLANE) * NSA_HD, (g % GROUPS_PER_PLANE + 1) * NSA_HD)
        s = lax.dot_general(qr, kbuf[slot, g][:, half].astype(BF16), NT_DIMS, preferred_element_type=F32)
        e, e_self, d = _softmax_with_self(s, jnp.sum(qf * ks_new, axis=-1, keepdims=True))
        o_s = (jnp.dot(e.astype(BF16), vbuf[slot, g][:, half].astype(BF16), preferred_element_type=F32)
               + e_self * vs_new) / d
        s = lax.dot_general(qr, win_ref[:, cols].astype(BF16), NT_DIMS, preferred_element_type=F32)
        e, e_self, d = _softmax_with_self(s, jnp.sum(qf * kw_new, axis=-1, keepdims=True), wmask)
        vw = win_ref[:, CMP_W + g * NSA_HD:CMP_W + (g + 1) * NSA_HD].astype(BF16)
        o_w = (jnp.dot(e.astype(BF16), vw, preferred_element_type=F32) + e_self * vw_new) / d
        o = gates[:, 0:1] * oc_ref[...] + gates[:, 1:2] * o_s + gates[:, 2:3] * o_w
        out = jnp.where(head_group == g, o, out)
    o_ref[...] = out


def nsa_sample_attend(cache, layer, page_table, idx, qr, new_rows, win, o_cmp, gates, win_first):
    B, n_pages = page_table.shape
    n_win = win.shape[2]
    hspec = lambda: pl.BlockSpec((None, NSA_HEADS, NSA_HD), lambda b, pt, ix: (b, 0, 0))
    return pl.pallas_call(
        functools.partial(_nsa_sample_attend_kernel, layer, n_pages, win_first),
        grid_spec=pltpu.PrefetchScalarGridSpec(
            num_scalar_prefetch=2,
            grid=(B,),
            in_specs=[pl.BlockSpec(memory_space=pl.ANY),
                      hspec(),
                      pl.BlockSpec((None, 4, CMP_W), lambda b, pt, ix: (b, 0, 0)),
                      pl.BlockSpec((None, None, n_win, 2 * CMP_W), lambda b, pt, ix: (layer, b, 0, 0)),
                      hspec(),
                      pl.BlockSpec((None, NSA_HEADS, LANES), lambda b, pt, ix: (b, 0, 0))],
            out_specs=hspec(),
            scratch_shapes=[pltpu.VMEM((2, NSA_KV, N_PAST_SEL * SEL_BLOCK, LANES), F32),
                            pltpu.VMEM((2, NSA_KV, N_PAST_SEL * SEL_BLOCK, LANES), F32),
                            pltpu.SemaphoreType.DMA((2,))]),
        out_shape=jax.ShapeDtypeStruct((B, NSA_HEADS, NSA_HD), F32),
        compiler_params=_cparams("arbitrary"),
        name="nsa_sample_attend",
    )(page_table.reshape(-1), idx.reshape(-1), cache, qr, new_rows, win, o_cmp, gates)


def _merge_kernel(oa_ref, ob_ref, wa_ref, wb_ref, ma_ref, mb_ref, o_ref, wa_bf, wb_bf):
    @pl.when(pl.program_id(1) == 0)
    def _():
        wa_bf[...] = wa_ref[...].astype(BF16)
        wb_bf[...] = wb_ref[...].astype(BF16)

    ya = jnp.dot(oa_ref[...], wa_bf[...], preferred_element_type=F32)
    yb = jnp.dot(ob_ref[...], wb_bf[...], preferred_element_type=F32)
    o_ref[...] = (_sigmoid(ma_ref[...]) * ya + _sigmoid(mb_ref[...]) * yb).astype(o_ref.dtype)


def merge(o_a, o_b, w_a, w_b, zm, layer, *, tm, tn=512):
    M, K = o_a.shape
    D = w_a.shape[-1]
    xspec = pl.BlockSpec((tm, K), lambda j, i: (i, 0))
    wspec = pl.BlockSpec((None, K, tn), lambda j, i: (layer, 0, j))
    return pl.pallas_call(
        _merge_kernel,
        grid=(D // tn, M // tm),
        in_specs=[xspec, xspec, wspec, wspec,
                  pl.BlockSpec((tm, tn), lambda j, i: (i, j)),
                  pl.BlockSpec((tm, tn), lambda j, i: (i, j + D // tn))],
        out_specs=pl.BlockSpec((tm, tn), lambda j, i: (i, j)),
        out_shape=jax.ShapeDtypeStruct((M, D), BF16),
        scratch_shapes=[pltpu.VMEM((K, tn), BF16), pltpu.VMEM((K, tn), BF16)],
        compiler_params=_cparams("arbitrary", "arbitrary"),
        name="merge",
    )(o_a, o_b, w_a, w_b, zm, zm)


MOE_TILE = 256
ROUTE_TILE = 256


def _first_argmax(vals):
    best, idx = vals[0], jnp.zeros(vals[0].shape, F32)
    for e in range(1, len(vals)):
        better = vals[e] > best
        idx = jnp.where(better, float(e), idx)
        best = jnp.where(better, vals[e], best)
    return best, idx


def _router_kernel(x_ref, w_ref, b_ref, o_ref):
    xs = _split3(x_ref[...])
    ws = _split3(w_ref[...])
    terms = ((0, 0), (0, 1), (1, 0), (0, 2), (2, 0), (1, 1))
    logits = b_ref[...] + sum(lax.dot_general(ws[a], xs[c], NT_DIMS, preferred_element_type=F32) for a, c in terms)
    m = jnp.max(logits, axis=0, keepdims=True)
    e = jnp.exp(logits - m)
    probs = e / jnp.sum(e, axis=0, keepdims=True)
    p = [probs[i:i + 1] for i in range(N_EXPERTS)]
    gscore = []
    for g in range(N_GROUPS):
        mem = p[g * EXPERTS_PER_GROUP:(g + 1) * EXPERTS_PER_GROUP]
        pairs = [mem[a] + mem[c] for a in range(EXPERTS_PER_GROUP) for c in range(a + 1, EXPERTS_PER_GROUP)]
        gscore.append(functools.reduce(jnp.maximum, pairs))
    _, g_sel = _first_argmax(gscore)
    masked = [jnp.where(g_sel == float(i // EXPERTS_PER_GROUP), p[i], -1.0) for i in range(N_EXPERTS)]
    v1, i1 = _first_argmax(masked)
    masked2 = [jnp.where(i1 == float(i), -2.0, masked[i]) for i in range(N_EXPERTS)]
    v2, i2 = _first_argmax(masked2)
    tot = v1 + v2
    o_ref[...] = jnp.concatenate([i1, i2, v1 / tot, v2 / tot, jnp.zeros((4,) + i1.shape[1:], F32)], axis=0)


def moe_route(x, w_router, b_router):
    T, D = x.shape
    return pl.pallas_call(
        _router_kernel,
        grid=(T // ROUTE_TILE,),
        in_specs=[pl.BlockSpec((ROUTE_TILE, D), lambda i: (i, 0)),
                  pl.BlockSpec((N_EXPERTS, D), lambda i: (0, 0)),
                  pl.BlockSpec((N_EXPERTS, 1), lambda i: (0, 0))],
        out_specs=pl.BlockSpec((8, ROUTE_TILE), lambda i: (0, i)),
        out_shape=jax.ShapeDtypeStruct((8, T), F32),
        compiler_params=_cparams("parallel"),
        name="moe_route",
    )(x, w_router.T, b_router.reshape(N_EXPERTS, 1))


GATHER_ROWS = 256


def _gather_copy(src_ref, dst_ref, sem, src_row, dst_row):
    return pltpu.make_async_copy(src_ref.at[pl.ds(src_row, 1)], dst_ref.at[pl.ds(dst_row, 1)], sem)


def _gather_rows_kernel(idx_ref, src_ref, dst_ref, sem):
    base = pl.program_id(0) * GATHER_ROWS

    def start(j, carry):
        _gather_copy(src_ref, dst_ref, sem, idx_ref[base + j], base + j).start()
        return carry

    def wait(j, carry):
        _gather_copy(src_ref, dst_ref, sem, 0, base + j).wait()
        return carry

    lax.fori_loop(0, GATHER_ROWS, start, 0, unroll=8)
    lax.fori_loop(0, GATHER_ROWS, wait, 0, unroll=8)


def gather_rows(src, idx):
    n = idx.shape[0]
    D = src.shape[1]
    assert n % GATHER_ROWS == 0
    src = src.reshape(src.shape[0], D // LANES, LANES)
    out = _gather_rows_call(src, idx, n)
    return out.reshape(n, D)


def _gather_rows_call(src, idx, n):
    return pl.pallas_call(
        _gather_rows_kernel,
        grid_spec=pltpu.PrefetchScalarGridSpec(
            num_scalar_prefetch=1,
            grid=(n // GATHER_ROWS,),
            in_specs=[pl.BlockSpec(memory_space=pl.ANY)],
            out_specs=pl.BlockSpec(memory_space=pl.ANY),
            scratch_shapes=[pltpu.SemaphoreType.DMA(())]),
        out_shape=jax.ShapeDtypeStruct((n,) + src.shape[1:], src.dtype),
        compiler_params=_cparams("arbitrary"),
        name="gather_rows",
    )(idx, src)


def _expert_ffn_kernel(te_ref, nt_ref, x_ref, w1_ref, w3_ref, w2_ref, o_ref):
    @pl.when(pl.program_id(0) < nt_ref[0])
    def _():
        x = x_ref[...]
        h = _silu(jnp.dot(x, w1_ref[...], preferred_element_type=F32)) * jnp.dot(x, w3_ref[...],
                                                                              preferred_element_type=F32)
        o_ref[...] = jnp.dot(h.astype(BF16), w2_ref[...], preferred_element_type=F32)

    @pl.when(pl.program_id(0) >= nt_ref[0])
    def _():
        o_ref[...] = jnp.zeros_like(o_ref)


def expert_ffn(x_sorted, tile_expert, n_tiles_used, w1, w3, w2, layer):
    P, D = x_sorted.shape
    F = w1.shape[-1]
    wspec = lambda a, c: pl.BlockSpec((None, None, a, c), lambda t, te, nt: (layer, te[t], 0, 0))
    return pl.pallas_call(
        _expert_ffn_kernel,
        grid_spec=pltpu.PrefetchScalarGridSpec(
            num_scalar_prefetch=2,
            grid=(P // MOE_TILE,),
            in_specs=[pl.BlockSpec((MOE_TILE, D), lambda t, te, nt: (t, 0)), wspec(D, F), wspec(D, F), wspec(F, D)],
            out_specs=pl.BlockSpec((MOE_TILE, D), lambda t, te, nt: (t, 0))),
        out_shape=jax.ShapeDtypeStruct((P, D), F32),
        compiler_params=_cparams("arbitrary"),
        name="expert_ffn",
    )(tile_expert, n_tiles_used, x_sorted, w1, w3, w2)


def _combine_ln_kernel(alpha, x_ref, y1_ref, y2_ref, wt_ref, g_ref, b_ref, o_ref, ob_ref):
    wt = wt_ref[...]
    y = wt[:, 0:1] * y1_ref[...] + wt[:, 1:2] * y2_ref[...]
    v = alpha * x_ref[...] + y
    mu = jnp.mean(v, axis=-1, keepdims=True)
    d = v - mu
    var = jnp.mean(d * d, axis=-1, keepdims=True)
    out = d * lax.rsqrt(var + LN_EPS) * g_ref[...] + b_ref[...]
    o_ref[...] = out
    ob_ref[...] = out.astype(BF16)


def moe_combine_layer_norm(x, ysel, wts, g, b, alpha, *, tm=256):
    T, D = x.shape
    nt = T // tm
    spec = pl.BlockSpec((tm, D), lambda i: (i, 0))
    vec = pl.BlockSpec((1, D), lambda i: (0, 0))
    return pl.pallas_call(
        functools.partial(_combine_ln_kernel, alpha),
        grid=(nt,),
        in_specs=[spec, spec, pl.BlockSpec((tm, D), lambda i: (i + nt, 0)),
                  pl.BlockSpec((tm, 2), lambda i: (i, 0)), vec, vec],
        out_specs=[spec, spec],
        out_shape=[jax.ShapeDtypeStruct((T, D), F32), jax.ShapeDtypeStruct((T, D), BF16)],
        compiler_params=_cparams("parallel"),
        name="moe_combine_ln",
    )(x, ysel, ysel, wts, g.reshape(1, D), b.reshape(1, D))


def moe_dispatch_plan(e1, e2, n_tiles):
    T = e1.shape[0]
    ea = jnp.concatenate([e1, e2])
    onehot = (ea[:, None] == jnp.arange(N_EXPERTS)[None, :]).astype(jnp.int32)
    rank = jnp.take_along_axis(jnp.cumsum(onehot, axis=0) - onehot, ea[:, None], axis=1)[:, 0]
    counts = onehot.sum(0)
    tiles = (counts + MOE_TILE - 1) // MOE_TILE
    tile_end = jnp.cumsum(tiles)
    dest = ((tile_end - tiles) * MOE_TILE)[ea] + rank
    src_token = jnp.zeros((n_tiles * MOE_TILE,), jnp.int32).at[dest].set(jnp.tile(jnp.arange(T, dtype=jnp.int32), 2))
    tile_expert = jnp.minimum(jnp.searchsorted(tile_end, jnp.arange(n_tiles), side='right'), N_EXPERTS - 1)
    return dest.astype(jnp.int32), src_token, tile_expert.astype(jnp.int32), tile_end[-1:].astype(jnp.int32)


def moe_block(x, xb, w_router, b_router, w1, w3, w2, layer, g, b, alpha):
    T = x.shape[0]
    r = moe_route(x, w_router, b_router)
    e1, e2 = r[0].astype(jnp.int32), r[1].astype(jnp.int32)
    n_tiles = 2 * T // MOE_TILE + N_EXPERTS
    dest, src_token, tile_expert, n_used = moe_dispatch_plan(e1, e2, n_tiles)
    x_sorted = gather_rows(xb, src_token)
    y_sorted = expert_ffn(x_sorted, tile_expert, n_used, w1, w3, w2, layer)
    ysel = gather_rows(y_sorted, dest)
    return moe_combine_layer_norm(x, ysel, r[2:4].T, g, b, alpha)


def _hgrn_sample_kernel(qt_ref, ft_ref, lbt_ref, onemt_ref, ia_ref, ga_ref, gn_ref, s0_ref, o_ref, s_ref):
    qa = qt_ref[...]
    fa = ft_ref[...]
    q_all = _silu(qa)
    sig = _sigmoid(fa)
    f_all = lbt_ref[...] + onemt_ref[...] * sig
    k_all = onemt_ref[...] * _sigmoid(-fa)
    for h in range(A_HEADS):
        cols = slice(h * A_DK, (h + 1) * A_DK)
        v = ia_ref[:, cols]
        s_new = f_all[:, h:h + 1] * s0_ref[h] + k_all[:, h:h + 1] * v
        s_ref[h] = s_new
        o = jnp.sum(s_new * q_all[:, h:h + 1], axis=0, keepdims=True)
        o_ref[:, cols] = _hgrn_gate_norm(o, ga_ref[:, cols], gn_ref[:, cols]).astype(o_ref.dtype)


def hgrn_sample(qa_t, fa_t, lb_t, onem_t, ia, ga, gnorm, state, layer):
    B = qa_t.shape[0]
    W = A_HEADS * A_DK
    col = pl.BlockSpec((None, A_DK, A_HEADS), lambda b: (b, 0, 0))
    par = pl.BlockSpec((A_DK, A_HEADS), lambda b: (0, 0))
    row = pl.BlockSpec((None, 1, W), lambda b: (b, 0, 0))
    return pl.pallas_call(
        _hgrn_sample_kernel,
        grid=(B,),
        in_specs=[col, col, par, par, row, row, pl.BlockSpec((1, W), lambda b: (0, 0)),
                  pl.BlockSpec((None, None, A_HEADS, A_DK, A_DK), lambda b: (layer, b, 0, 0, 0))],
        out_specs=[row, pl.BlockSpec((None, A_HEADS, A_DK, A_DK), lambda b: (b, 0, 0, 0))],
        out_shape=[jax.ShapeDtypeStruct((B, 1, W), BF16), jax.ShapeDtypeStruct((B, A_HEADS, A_DK, A_DK), F32)],
        compiler_params=_cparams("parallel"),
        name="hgrn_sample",
    )(qa_t, fa_t, lb_t, onem_t, ia, ga, gnorm, state)


MM_TILE_M = 768
MM_TILE_N = 512
W_A = A_HEADS * A_DK
W_Q = NSA_HEADS * NSA_HD
COL_QB = 4 * W_A
COL_KV = COL_QB + W_Q
COL_GB = COL_KV + 6 * CMP_W
COL_MA = COL_GB + 3 * NSA_HEADS


def _pad_lanes(a):
    return jnp.pad(a, [(0, 0)] * (a.ndim - 1) + [(0, LANES - a.shape[-1])])


def kernel(x_prompt, x_sample, cache_nsa_kv, cache_nsa_win, state_hgrn, page_table, w_in, lb_raw, gnorm_a, cmp_pe, cmp_w1, cmp_w2, w_branch_a, w_branch_b, w_out, ln1_g, ln1_b, ln2_g, ln2_b, w_router, b_router, w_e1, w_e3, w_e2):
    depth, D = w_in.shape[0], w_in.shape[1]
    L, B = x_prompt.shape[1], x_sample.shape[0]
    assert x_prompt.shape[0] == 1 and x_sample.shape[1] == 1
    n_pool = cache_nsa_kv.shape[1]
    past_len = page_table.shape[1] * PAGE_SIZE
    win_rows = cache_nsa_win.shape[2]
    T = L + B
    M = -(-T // MM_TILE_M) * MM_TILE_M
    alpha = (2 * depth) ** 0.25
    prompt, sample = slice(0, L), slice(L, T)

    lbs = jnp.cumsum(jax.nn.softmax(lb_raw.astype(F32), axis=0), axis=0)
    lbs = lbs - lbs[0:1]
    pos = jnp.concatenate([jnp.arange(L), jnp.full((B,), past_len), jnp.zeros((M - T,), jnp.int32)])
    cos, sin = rope_tables(pos)
    ones, zeros = jnp.ones_like(cos), jnp.zeros_like(cos)
    cache = cache_nsa_kv.reshape(depth, n_pool, PAGE_SIZE, 4 * CMP_W)
    win_cache = cache_nsa_win.reshape(depth, B, win_rows, 2 * CMP_W)
    w1b, w3b, w2b = w_e1.astype(BF16), w_e3.astype(BF16), w_e2.astype(BF16)
    scale = NSA_HD ** -0.5

    x = jnp.concatenate([x_prompt[0], x_sample[:, 0], jnp.zeros((M - T, D), F32)])
    xb = x.astype(BF16)
    heads_major = lambda a, h: a.reshape(a.shape[0], h, NSA_HD).transpose(1, 0, 2)
    col_major = lambda a: a.reshape(B, A_HEADS, A_DK).transpose(0, 2, 1)
    outs = [[] for _ in range(6)]
    for l in range(depth):
        lb = lbs[l]
        mm = functools.partial(matmul, tm=MM_TILE_M, tn=MM_TILE_N)
        z = mm(xb, w_in, l, col0=0, ncols=COL_GB)
        zg = matmul(xb, _pad_lanes(w_in[l, :, COL_GB:COL_MA])[None], 0, tm=MM_TILE_M, tn=LANES)
        zm = mm(xb, w_in[l, :, COL_MA:][None], 0)

        o_a_p, st_p = hgrn_prompt(z, L, jnp.log(lb)[None], jnp.log1p(-lb)[None], (1.0 - lb)[None], gnorm_a[l][None])
        zs = z[sample]
        o_a_s, st_s = hgrn_sample(col_major(zs[:, :W_A]), col_major(zs[:, W_A:2 * W_A]),
                                  lb.reshape(A_HEADS, A_DK).T, (1.0 - lb).reshape(A_HEADS, A_DK).T,
                                  zs[:, None, 2 * W_A:3 * W_A], zs[:, None, 3 * W_A:4 * W_A], gnorm_a[l][None],
                                  state_hgrn, l)
        o_a = jnp.concatenate([o_a_p, o_a_s[:, 0], jnp.zeros((M - T, W_A), BF16)])

        q_plain = rope(z, COL_QB, W_Q, ones, zeros, scale=scale, out_dtype=BF16)
        q_rot = rope(z, COL_QB, W_Q, cos, sin, scale=scale, out_dtype=BF16)
        ks_rot = rope(z, COL_KV + 2 * CMP_W, CMP_W, cos, sin)
        kw_rot = rope(z, COL_KV + 4 * CMP_W, CMP_W, cos, sin)
        v_sel = z[:, COL_KV + 3 * CMP_W:COL_KV + 4 * CMP_W]
        v_win = z[:, COL_KV + 5 * CMP_W:COL_KV + 6 * CMP_W]
        rows = jnp.concatenate([z[:, COL_KV:COL_KV + 2 * CMP_W], ks_rot, v_sel], axis=-1)
        win_new = jnp.concatenate([kw_rot, v_win], axis=-1)
        cw = compress_weights(cmp_pe[l], cmp_w1[l], cmp_w2[l])

        kc_p, vc_p = compress_paged(rows[prompt].reshape(1, L // PAGE_SIZE, PAGE_SIZE, 4 * CMP_W), 0,
                                    jnp.arange(L // PAGE_SIZE, dtype=jnp.int32)[None], *cw)
        grp = lambda a: heads_major(a, NSA_KV).astype(BF16)
        gates_p = _pad_lanes(zg[prompt, :3 * NSA_HEADS].reshape(L, NSA_KV, 3 * NSA_REP).transpose(1, 0, 2))
        o_b_p = nsa_prompt(heads_major(q_plain[prompt], NSA_HEADS), heads_major(q_rot[prompt], NSA_HEADS),
                           grp(kc_p[0]), grp(vc_p[0]), grp(ks_rot[prompt]), grp(v_sel[prompt]),
                           grp(kw_rot[prompt]), grp(v_win[prompt]), gates_p)

        kc_s, vc_s = compress_paged(cache, l, page_table, *cw)
        o_c_s, idx = nsa_sample_select(q_plain[sample].reshape(B, NSA_HEADS, NSA_HD), kc_s, vc_s,
                                       past_len, past_len // SEL_BLOCK)
        rs, ws = rows[sample], win_new[sample]
        new_rows = jnp.stack([rs[:, 2 * CMP_W:3 * CMP_W], rs[:, 3 * CMP_W:], ws[:, :CMP_W], ws[:, CMP_W:]], axis=1)
        gates_s = _pad_lanes(zg[sample, :3 * NSA_HEADS].reshape(B, NSA_HEADS, 3))
        o_b_s = nsa_sample_attend(cache, l, page_table, idx[:, :, :N_SEL], q_rot[sample].reshape(B, NSA_HEADS, NSA_HD),
                                  new_rows, win_cache, o_c_s, gates_s, win_rows - WINDOW + 1)
        o_b = jnp.concatenate([o_b_p, o_b_s.reshape(B, W_Q).astype(BF16), jnp.zeros((M - T, W_Q), BF16)])

        h = merge(o_a, o_b, w_branch_a, w_branch_b, zm, l, tm=MM_TILE_M)
        y = mm(h, w_out, l)
        x1, x1b = residual_layer_norm(x, y, ln1_g[l], ln1_b[l], alpha)
        x, xb = moe_block(x1, x1b, w_router, b_router, w1b, w3b, w2b, l, ln2_g[l], ln2_b[l], alpha)

        wbp = min(WINDOW, L)
        outs[0].append(rows[prompt].reshape(1, L, 4, NSA_KV, NSA_HD))
        outs[1].append(rs.reshape(B, 1, 4, NSA_KV, NSA_HD))
        outs[2].append(win_new[L - wbp:L].reshape(1, wbp, 2, NSA_KV, NSA_HD))
        outs[3].append(jnp.concatenate([cache_nsa_win[l][:, 1:], ws.reshape(B, 1, 2, NSA_KV, NSA_HD)], axis=1))
        outs[4].append(st_p[None])
        outs[5].append(st_s)
    return (x[prompt][None], x[sample][:, None]) + tuple(jnp.stack(o) for o in outs)
```

```python
import functools
import math

import jax
import jax.numpy as jnp
import numpy as np
from jax import lax
from jax.experimental import pallas as pl
from jax.experimental.pallas import tpu as pltpu

F32 = jnp.float32
BF16 = jnp.bfloat16

A_HEADS = 8
A_DK = 128
NSA_HEADS = 16
NSA_KV = 4
NSA_REP = NSA_HEADS // NSA_KV
NSA_HD = 64
CMP_BLOCK = 32
CMP_STRIDE = 16
CMP_HIDDEN = 128
SEL_BLOCK = 64
N_SEL = 16
WINDOW = 512
PAGE_SIZE = 128
ROPE_THETA = 10000.0
N_EXPERTS = 16
N_GROUPS = 4
EXPERTS_PER_GROUP = N_EXPERTS // N_GROUPS
LN_EPS = 1e-5
NEG = -1e30

VMEM_LIMIT_BYTES = 56 * 1024 * 1024
LANES = 128


def _cparams(*sem):
    return pltpu.CompilerParams(dimension_semantics=sem, vmem_limit_bytes=VMEM_LIMIT_BYTES)


def _sigmoid(x):
    return 1.0 / (1.0 + jnp.exp(-x))


def _silu(x):
    return x * _sigmoid(x)


def _mm_kernel(x_ref, w_ref, o_ref, wb_ref):
    @pl.when(pl.program_id(1) == 0)
    def _():
        wb_ref[...] = w_ref[...].astype(BF16)

    o_ref[...] = jnp.dot(x_ref[...], wb_ref[...], preferred_element_type=F32).astype(o_ref.dtype)


def matmul(x, w, layer, *, tm, tn, col0=0, ncols=None, out_dtype=F32):
    M, K = x.shape
    ncols = w.shape[-1] if ncols is None else ncols
    assert M % tm == 0 and ncols % tn == 0 and col0 % tn == 0
    off = col0 // tn
    return pl.pallas_call(
        _mm_kernel,
        grid=(ncols // tn, M // tm),
        in_specs=[pl.BlockSpec((tm, K), lambda j, i: (i, 0)),
                  pl.BlockSpec((None, K, tn), lambda j, i: (layer, 0, j + off))],
        out_specs=pl.BlockSpec((tm, tn), lambda j, i: (i, j)),
        out_shape=jax.ShapeDtypeStruct((M, ncols), out_dtype),
        scratch_shapes=[pltpu.VMEM((K, tn), BF16)],
        compiler_params=_cparams("arbitrary", "arbitrary"),
        name="matmul",
    )(x, w)


def _ln_kernel(alpha, x_ref, y_ref, g_ref, b_ref, o_ref, ob_ref):
    v = alpha * x_ref[...] + y_ref[...]
    mu = jnp.mean(v, axis=-1, keepdims=True)
    d = v - mu
    var = jnp.mean(d * d, axis=-1, keepdims=True)
    out = d * lax.rsqrt(var + LN_EPS) * g_ref[...] + b_ref[...]
    o_ref[...] = out
    ob_ref[...] = out.astype(BF16)


def residual_layer_norm(x, y, g, b, alpha, *, tm=256):
    M, D = x.shape
    spec = pl.BlockSpec((tm, D), lambda i: (i, 0))
    vec = pl.BlockSpec((1, D), lambda i: (0, 0))
    return pl.pallas_call(
        functools.partial(_ln_kernel, alpha),
        grid=(M // tm,),
        in_specs=[spec, spec, vec, vec],
        out_specs=[spec, spec],
        out_shape=[jax.ShapeDtypeStruct((M, D), F32), jax.ShapeDtypeStruct((M, D), BF16)],
        compiler_params=_cparams("parallel"),
        name="residual_ln",
    )(x, y, g.reshape(1, D), b.reshape(1, D))


def _rope_kernel(scale, x_ref, cos_ref, sin_ref, o_ref):
    cos = cos_ref[...]
    sin = sin_ref[...]
    lane = lax.broadcasted_iota(jnp.int32, cos.shape, 1)
    first_half = (lane % NSA_HD) < (NSA_HD // 2)
    for j in range(x_ref.shape[1] // LANES):
        x = x_ref[:, j * LANES:(j + 1) * LANES]
        partner = jnp.where(first_half, pltpu.roll(x, LANES - NSA_HD // 2, 1), pltpu.roll(x, NSA_HD // 2, 1))
        o_ref[:, j * LANES:(j + 1) * LANES] = ((x * cos + partner * sin) * scale).astype(o_ref.dtype)


def rope(z, col0, width, cos, sin, *, scale=1.0, out_dtype=F32, tm=256):
    M = z.shape[0]
    assert col0 % width == 0 and width % LANES == 0
    return pl.pallas_call(
        functools.partial(_rope_kernel, scale),
        grid=(M // tm,),
        in_specs=[pl.BlockSpec((tm, width), lambda i: (i, col0 // width)),
                  pl.BlockSpec((tm, LANES), lambda i: (i, 0)),
                  pl.BlockSpec((tm, LANES), lambda i: (i, 0))],
        out_specs=pl.BlockSpec((tm, width), lambda i: (i, 0)),
        out_shape=jax.ShapeDtypeStruct((M, width), out_dtype),
        compiler_params=_cparams("parallel"),
        name="rope",
    )(z, cos, sin)


def rope_tables(pos):
    half = NSA_HD // 2
    inv = jnp.power(ROPE_THETA, -jnp.arange(half, dtype=F32) / half)
    ang = pos.astype(F32)[:, None] * inv[None, :]
    cos, sin = jnp.cos(ang), jnp.sin(ang)
    reps = LANES // NSA_HD
    return jnp.tile(jnp.concatenate([cos, cos], -1), (1, reps)), jnp.tile(jnp.concatenate([-sin, sin], -1), (1, reps))


HG_CHUNK = 64
HG_SUB = 16
HG_BLOCK = 256


def _split3(x):
    hi = x.astype(BF16)
    r1 = x - hi.astype(F32)
    mid = r1.astype(BF16)
    lo = (r1 - mid.astype(F32)).astype(BF16)
    return hi, mid, lo


def _log_forget(fa, loglb, log1mlb):
    log_sig = jnp.minimum(fa, 0.0) - jnp.log(1.0 + jnp.exp(-jnp.abs(fa)))
    b = log1mlb + log_sig
    mx = jnp.maximum(loglb, b)
    return mx + jnp.log(1.0 + jnp.exp(-jnp.abs(loglb - b)))


def _hgrn_gate_norm(o, ga, gn):
    o = o * lax.rsqrt(jnp.mean(o * o, axis=-1, keepdims=True) + 1e-6)
    return o * gn * _silu(ga)


def _hgrn_prompt_kernel(qa_ref, fa_ref, ia_ref, ga_ref, loglb_ref, log1mlb_ref, onemlb_ref, gn_ref,
                        o_ref, s_ref, st_ref):
    c = pl.program_id(1)

    @pl.when(c == 0)
    def _():
        st_ref[...] = jnp.zeros_like(st_ref)

    loglb, log1mlb, onemlb, gn = loglb_ref[...], log1mlb_ref[...], onemlb_ref[...], gn_ref[...]
    row = lax.broadcasted_iota(jnp.int32, (HG_CHUNK, HG_CHUNK), 0)
    col = lax.broadcasted_iota(jnp.int32, (HG_CHUNK, HG_CHUNK), 1)
    tril = (col <= row).astype(BF16)
    srow = lax.broadcasted_iota(jnp.int32, (HG_SUB, HG_SUB), 0)
    scol = lax.broadcasted_iota(jnp.int32, (HG_SUB, HG_SUB), 1)

    def chunk(ci, carry):
        r0 = pl.multiple_of(ci * HG_CHUNK, HG_CHUNK)
        rows = pl.ds(r0, HG_CHUNK)
        qa, fa, v = qa_ref[rows, :], fa_ref[rows, :], ia_ref[rows, :]
        q = _silu(qa)
        logf = _log_forget(fa, loglb, log1mlb)
        k = onemlb * _sigmoid(-fa)
        b = sum(jnp.dot(tril, t, preferred_element_type=F32) for t in _split3(logf))
        st = st_ref[...]
        vb = v.astype(BF16)
        o_inter = lax.dot_general((q * jnp.exp(b)).astype(BF16), st.astype(BF16), (((1,), (1,)), ((), ())),
                                  preferred_element_type=F32)
        for i in range(HG_CHUNK // HG_SUB):
            lo = i * HG_SUB
            qi, bi, ki = q[lo:lo + HG_SUB], b[lo:lo + HG_SUB], k[lo:lo + HG_SUB]
            o = o_inter[lo:lo + HG_SUB]
            if i > 0:
                anchor = b[lo - 1:lo]
                qt = (qi * jnp.exp(bi - anchor)).astype(BF16)
                kt = (k[:lo] * jnp.exp(anchor - b[:lo])).astype(BF16)
                a_off = lax.dot_general(qt, kt, (((1,), (1,)), ((), ())), preferred_element_type=F32)
                o = o + jnp.dot(a_off.astype(BF16), vb[:lo], preferred_element_type=F32)
            a_d = jnp.zeros((HG_SUB, HG_SUB), F32)
            for s in range(HG_SUB):
                p = qi * ki[s:s + 1] * jnp.exp(jnp.minimum(bi - bi[s:s + 1], 0.0))
                a_d = jnp.where(scol == s, jnp.sum(p, axis=-1, keepdims=True), a_d)
            a_d = jnp.where(scol <= srow, a_d, 0.0)
            o = o + jnp.dot(a_d.astype(BF16), vb[lo:lo + HG_SUB], preferred_element_type=F32)
            ga = ga_ref[pl.ds(r0 + lo, HG_SUB), :]
            o_ref[pl.ds(r0 + lo, HG_SUB), :] = _hgrn_gate_norm(o, ga, gn).astype(o_ref.dtype)
        b_last = b[HG_CHUNK - 1:HG_CHUNK]
        kt = (k * jnp.exp(b_last - b)).astype(BF16)
        st_ref[...] = st * jnp.exp(b_last) + lax.dot_general(vb, kt, (((0,), (0,)), ((), ())),
                                                             preferred_element_type=F32)
        return carry

    lax.fori_loop(0, HG_BLOCK // HG_CHUNK, chunk, 0)

    @pl.when(c == pl.num_programs(1) - 1)
    def _():
        s_ref[...] = st_ref[...].T


def hgrn_prompt(z, L, loglb, log1mlb, onemlb, gnorm):
    blk = lambda part: pl.BlockSpec((HG_BLOCK, A_DK), lambda h, c: (c, part * A_HEADS + h))
    vec = pl.BlockSpec((1, A_DK), lambda h, c: (0, h))
    return pl.pallas_call(
        _hgrn_prompt_kernel,
        grid=(A_HEADS, L // HG_BLOCK),
        in_specs=[blk(0), blk(1), blk(2), blk(3), vec, vec, vec, vec],
        out_specs=[pl.BlockSpec((HG_BLOCK, A_DK), lambda h, c: (c, h)),
                   pl.BlockSpec((None, A_DK, A_DK), lambda h, c: (h, 0, 0))],
        out_shape=[jax.ShapeDtypeStruct((L, A_HEADS * A_DK), BF16),
                   jax.ShapeDtypeStruct((A_HEADS, A_DK, A_DK), F32)],
        scratch_shapes=[pltpu.VMEM((A_DK, A_DK), F32)],
        compiler_params=_cparams("arbitrary", "arbitrary"),
        name="hgrn_prompt",
    )(z, z, z, z, loglb, log1mlb, onemlb, gnorm)


CMP_PAGES = 16
CMP_ROWS = CMP_PAGES * PAGE_SIZE
CMP_OUT = CMP_ROWS // CMP_STRIDE
CMP_W = NSA_KV * NSA_HD
CMP_PLANES = 2 * CMP_W // LANES
GROUPS_PER_PLANE = LANES // NSA_HD


def _compress_copies(pt_ref, cache_ref, xbuf, sem, layer, bb, cc, slot, n_pages):
    out = []
    nxt = jnp.minimum((cc + 1) * CMP_PAGES, n_pages - 1)
    for q in range(CMP_PLANES):
        cols = pl.ds(q * LANES, LANES)
        for j in range(CMP_PAGES):
            pg = pt_ref[bb, cc * CMP_PAGES + j]
            out.append(pltpu.make_async_copy(cache_ref.at[layer, pg, :, cols],
                                             xbuf.at[slot, q, pl.ds(j * PAGE_SIZE, PAGE_SIZE), :], sem.at[slot]))
        out.append(pltpu.make_async_copy(cache_ref.at[layer, pt_ref[bb, nxt], pl.ds(0, CMP_STRIDE), cols],
                                         xbuf.at[slot, q, pl.ds(CMP_ROWS, CMP_STRIDE), :], sem.at[slot]))
    return out


def _compress_kernel(layer, pt_ref, cache_ref, w1_hbm, pe_ref, w2_ref, kc_ref, vc_ref, xbuf, w1buf, sem, wsem):
    b, c = pl.program_id(0), pl.program_id(1)
    nb, nc = pl.num_programs(0), pl.num_programs(1)
    step = b * nc + c
    slot = step % 2
    n_pages = nc * CMP_PAGES
    mk = functools.partial(_compress_copies, pt_ref, cache_ref, xbuf, sem, layer)

    @pl.when(step == 0)
    def _():
        wcopy = pltpu.make_async_copy(w1_hbm, w1buf, wsem)
        wcopy.start()
        for cp in mk(b, c, slot, n_pages):
            cp.start()
        wcopy.wait()

    @pl.when(step + 1 < nb * nc)
    def _():
        last_c = c == nc - 1
        for cp in mk(jnp.where(last_c, b + 1, b), jnp.where(last_c, 0, c + 1), 1 - slot, n_pages):
            cp.start()

    for cp in mk(b, c, slot, n_pages):
        cp.wait()

    planes_per_out = CMP_PLANES // 2
    for t, o_ref in enumerate((kc_ref, vc_ref)):
        for q in range(planes_per_out):
            xs = xbuf.at[slot, t * planes_per_out + q]
            acc = jnp.zeros((CMP_OUT, GROUPS_PER_PLANE * CMP_HIDDEN), F32)
            for p in range(CMP_BLOCK):
                x = xs[pl.ds(p, CMP_OUT, stride=CMP_STRIDE), :] + pe_ref[t, p]
                acc = acc + jnp.dot(x.astype(BF16), w1buf[t, p], preferred_element_type=F32)
            o_ref[:, q * LANES:(q + 1) * LANES] = jnp.dot(_silu(acc).astype(BF16), w2_ref[t],
                                                          preferred_element_type=F32)


def compress_weights(pe, w1, w2):
    eye = jnp.eye(GROUPS_PER_PLANE, dtype=F32)
    w1r = w1.reshape(2, CMP_BLOCK, NSA_HD, CMP_HIDDEN)
    w1bd = jnp.einsum('gh,tpdn->tpgdhn', eye, w1r).reshape(2, CMP_BLOCK, LANES, GROUPS_PER_PLANE * CMP_HIDDEN)
    w2bd = jnp.einsum('gh,tnd->tgnhd', eye, w2).reshape(2, GROUPS_PER_PLANE * CMP_HIDDEN, LANES)
    pet = jnp.tile(pe, (1, 1, GROUPS_PER_PLANE)).reshape(2, CMP_BLOCK, 1, LANES)
    return w1bd.astype(BF16), pet, w2bd.astype(BF16)


def compress_paged(cache, layer, page_table, w1bd, pet, w2bd):
    B, n_pages = page_table.shape
    assert n_pages % CMP_PAGES == 0
    nc = n_pages // CMP_PAGES
    out = jax.ShapeDtypeStruct((B, nc * CMP_OUT, CMP_W), F32)
    ospec = pl.BlockSpec((None, CMP_OUT, CMP_W), lambda b, c, pt: (b, c, 0))
    return pl.pallas_call(
        functools.partial(_compress_kernel, layer),
        grid_spec=pltpu.PrefetchScalarGridSpec(
            num_scalar_prefetch=1,
            grid=(B, nc),
            in_specs=[pl.BlockSpec(memory_space=pl.ANY),
                      pl.BlockSpec(memory_space=pl.ANY),
                      pl.BlockSpec(pet.shape, lambda b, c, pt: (0, 0, 0, 0)),
                      pl.BlockSpec(w2bd.shape, lambda b, c, pt: (0, 0, 0))],
            out_specs=[ospec, ospec],
            scratch_shapes=[pltpu.VMEM((2, CMP_PLANES, CMP_ROWS + CMP_STRIDE, LANES), F32),
                            pltpu.VMEM(w1bd.shape, BF16),
                            pltpu.SemaphoreType.DMA((2,)),
                            pltpu.SemaphoreType.DMA(())]),
        out_shape=[out, out],
        compiler_params=_cparams("arbitrary", "arbitrary"),
        name="nsa_compress",
    )(page_table, cache, w1bd, pet, w2bd)


NSA_TQ = 256
NSA_TK = 512
SEL_PER_CMP = SEL_BLOCK // CMP_STRIDE
NT_DIMS = (((1,), (1,)), ((), ()))


def _masked_softmax(s, mask):
    s = jnp.where(mask, s, NEG)
    e = jnp.exp(s - jnp.max(s, axis=-1, keepdims=True)) * mask.astype(F32)
    d = jnp.sum(e, axis=-1, keepdims=True)
    return e / jnp.where(d > 0, d, 1.0)


def _pool_matrix(n_cmp, n_sel):
    i = lax.broadcasted_iota(jnp.int32, (n_cmp, n_sel), 0)
    j = lax.broadcasted_iota(jnp.int32, (n_cmp, n_sel), 1)
    lo = SEL_PER_CMP * j - (CMP_BLOCK // CMP_STRIDE - 1)
    return ((i >= lo) & (i <= SEL_PER_CMP * j + SEL_PER_CMP - 1)).astype(BF16)


def _importance(psum, pool):
    return sum(jnp.dot(t, pool, preferred_element_type=F32) for t in _split3(psum))


def _top_k_mask(score, k):
    lane = lax.broadcasted_iota(jnp.int32, score.shape, 1).astype(F32)
    sel = jnp.zeros(score.shape, F32)
    for _ in range(k):
        m = jnp.max(score, axis=-1, keepdims=True)
        idx = jnp.min(jnp.where(score == m, lane, float(score.shape[-1])), axis=-1, keepdims=True)
        hit = lane == idx
        sel = jnp.where(hit, 1.0, sel)
        score = jnp.where(hit, -jnp.inf, score)
    return sel


def _nsa_prompt_kernel(q_ref, qr_ref, kc_ref, vc_ref, ks_ref, vs_ref, kw_ref, vw_ref, gate_ref, o_ref,
                       sel_ref, m_ref, l_ref, acc_ref, oc_ref):
    tq, tk = NSA_TQ, NSA_TK
    s0 = pl.program_id(1) * tq
    tpos = s0 + lax.broadcasted_iota(jnp.int32, (tq, 1), 0)
    n_cmp = kc_ref.shape[0]
    n_sel = sel_ref.shape[1]

    cidx = lax.broadcasted_iota(jnp.int32, (tq, n_cmp), 1)
    cmask = (cidx * CMP_STRIDE + (CMP_BLOCK - 1) <= tpos) & (cidx < n_cmp - 1)
    psum = jnp.zeros((tq, n_cmp), F32)
    for r in range(NSA_REP):
        s = lax.dot_general(q_ref[r], kc_ref[...], NT_DIMS, preferred_element_type=F32)
        p = _masked_softmax(s, cmask)
        oc_ref[r] = jnp.dot(p.astype(BF16), vc_ref[...], preferred_element_type=F32)
        psum = psum + p

    imp = _importance(psum, _pool_matrix(n_cmp, n_sel))
    blk = lax.broadcasted_iota(jnp.int32, (tq, n_sel), 1)
    valid = blk * SEL_BLOCK <= tpos
    forced = (blk == 0) | (blk == tpos // SEL_BLOCK)
    score = jnp.where(forced, jnp.inf, jnp.where(valid, imp, -jnp.inf))
    sel_ref[...] = _top_k_mask(score, N_SEL).astype(BF16)

    m_ref[...] = jnp.full(m_ref.shape, NEG, F32)
    l_ref[...] = jnp.zeros(l_ref.shape, F32)
    acc_ref[...] = jnp.zeros(acc_ref.shape, F32)

    def sweep(kt, carry):
        k0 = pl.multiple_of(kt * tk, tk)
        jj = lax.broadcasted_iota(jnp.int32, (n_sel, tk), 0)
        ss = lax.broadcasted_iota(jnp.int32, (n_sel, tk), 1)
        expand = (jj == kt * (tk // SEL_BLOCK) + ss // SEL_BLOCK).astype(BF16)
        chosen = jnp.dot(sel_ref[...], expand, preferred_element_type=F32)
        kpos = k0 + lax.broadcasted_iota(jnp.int32, (tq, tk), 1)
        allowed = (chosen > 0.5) & (kpos <= tpos)
        ks = ks_ref[pl.ds(k0, tk), :]
        vs = vs_ref[pl.ds(k0, tk), :]
        for r in range(NSA_REP):
            s = lax.dot_general(qr_ref[r], ks, NT_DIMS, preferred_element_type=F32)
            s = jnp.where(allowed, s, NEG)
            m_old = m_ref[r]
            m_new = jnp.maximum(m_old, jnp.max(s, axis=-1, keepdims=True))
            p = jnp.exp(s - m_new)
            alpha = jnp.exp(m_old - m_new)
            l_ref[r] = alpha * l_ref[r] + jnp.sum(p, axis=-1, keepdims=True)
            acc_ref[r] = alpha * acc_ref[r] + jnp.dot(p.astype(BF16), vs, preferred_element_type=F32)
            m_ref[r] = m_new
        return carry

    lax.fori_loop(0, (s0 + tq + tk - 1) // tk, sweep, 0)

    band = WINDOW + tq
    w0 = pl.multiple_of(jnp.maximum(s0 - WINDOW, 0), tq)
    wpos = w0 + lax.broadcasted_iota(jnp.int32, (tq, band), 1)
    wmask = (wpos <= tpos) & (wpos > tpos - WINDOW)
    kw = kw_ref[pl.ds(w0, band), :]
    vw = vw_ref[pl.ds(w0, band), :]
    gates = _sigmoid(gate_ref[...])
    for r in range(NSA_REP):
        s = lax.dot_general(qr_ref[r], kw, NT_DIMS, preferred_element_type=F32)
        o_w = jnp.dot(_masked_softmax(s, wmask).astype(BF16), vw, preferred_element_type=F32)
        o_s = acc_ref[r] / l_ref[r]
        o = (gates[:, 3 * r:3 * r + 1] * oc_ref[r] + gates[:, 3 * r + 1:3 * r + 2] * o_s
             + gates[:, 3 * r + 2:3 * r + 3] * o_w)
        o_ref[:, r * NSA_HD:(r + 1) * NSA_HD] = o.astype(o_ref.dtype)


def nsa_prompt(q, qr, kc, vc, ks, vs, kw, vw, gates):
    L = q.shape[1]
    n_cmp = kc.shape[1]
    n_sel = L // SEL_BLOCK
    assert L % NSA_TK == 0 and n_sel % LANES == 0 or n_sel <= LANES
    qspec = pl.BlockSpec((NSA_REP, NSA_TQ, NSA_HD), lambda g, i: (g, i, 0))
    cspec = pl.BlockSpec((None, n_cmp, NSA_HD), lambda g, i: (g, 0, 0))
    kspec = pl.BlockSpec((None, L, NSA_HD), lambda g, i: (g, 0, 0))
    return pl.pallas_call(
        _nsa_prompt_kernel,
        grid=(NSA_KV, L // NSA_TQ),
        in_specs=[qspec, qspec, cspec, cspec, kspec, kspec, kspec, kspec,
                  pl.BlockSpec((None, NSA_TQ, LANES), lambda g, i: (g, i, 0))],
        out_specs=pl.BlockSpec((NSA_TQ, NSA_REP * NSA_HD), lambda g, i: (i, g)),
        out_shape=jax.ShapeDtypeStruct((L, NSA_HEADS * NSA_HD), BF16),
        scratch_shapes=[pltpu.VMEM((NSA_TQ, n_sel), BF16),
                        pltpu.VMEM((NSA_REP, NSA_TQ, 1), F32),
                        pltpu.VMEM((NSA_REP, NSA_TQ, 1), F32),
                        pltpu.VMEM((NSA_REP, NSA_TQ, NSA_HD), F32),
                        pltpu.VMEM((NSA_REP, NSA_TQ, NSA_HD), F32)],
        compiler_params=_cparams("parallel", "arbitrary"),
        name="nsa_prompt",
    )(q, qr, kc, vc, ks, vs, kw, vw, gates)


def _top_k_indices(score, k):
    lane = lax.broadcasted_iota(jnp.int32, score.shape, 1).astype(F32)
    out = jnp.zeros(score.shape, F32)
    for t in range(k):
        m = jnp.max(score, axis=-1, keepdims=True)
        idx = jnp.min(jnp.where(score == m, lane, float(score.shape[-1])), axis=-1, keepdims=True)
        out = jnp.where(lane == float(t), idx, out)
        score = jnp.where(lane == idx, -jnp.inf, score)
    return out.astype(jnp.int32)


def _nsa_sample_select_kernel(q_pos, q_ref, kc_ref, vc_ref, oc_ref, idx_ref):
    n_cmp = kc_ref.shape[0]
    n_sel = idx_ref.shape[-1]
    q = q_ref[...]
    head_group = lax.broadcasted_iota(jnp.int32, (NSA_HEADS, 1), 0) // NSA_REP
    cidx = lax.broadcasted_iota(jnp.int32, (NSA_HEADS, n_cmp), 1)
    cmask = (cidx * CMP_STRIDE + (CMP_BLOCK - 1) <= q_pos) & (cidx < n_cmp - 1)
    grp_row = lax.broadcasted_iota(jnp.int32, (NSA_KV, 1), 0)
    oc = jnp.zeros((NSA_HEADS, NSA_HD), F32)
    psum = jnp.zeros((NSA_KV, n_cmp), F32)
    for g in range(NSA_KV):
        cols = slice(g * NSA_HD, (g + 1) * NSA_HD)
        s = lax.dot_general(q, kc_ref[:, cols].astype(BF16), NT_DIMS, preferred_element_type=F32)
        p = _masked_softmax(s, cmask)
        o = jnp.dot(p.astype(BF16), vc_ref[:, cols].astype(BF16), preferred_element_type=F32)
        mine = head_group == g
        oc = jnp.where(mine, o, oc)
        pg = jnp.sum(jnp.where(mine, p, 0.0), axis=0, keepdims=True)
        psum = jnp.where(grp_row == g, pg, psum)
    oc_ref[...] = oc
    imp = _importance(psum, _pool_matrix(n_cmp, n_sel))
    blk = lax.broadcasted_iota(jnp.int32, (NSA_KV, n_sel), 1)
    score = jnp.where(blk == 0, jnp.inf, imp)
    idx_ref[...] = _top_k_indices(score, N_SEL - 1)


def nsa_sample_select(q, kc, vc, q_pos, n_sel):
    B, n_cmp = kc.shape[0], kc.shape[1]
    cspec = pl.BlockSpec((None, n_cmp, CMP_W), lambda b: (b, 0, 0))
    hspec = pl.BlockSpec((None, NSA_HEADS, NSA_HD), lambda b: (b, 0, 0))
    return pl.pallas_call(
        functools.partial(_nsa_sample_select_kernel, q_pos),
        grid=(B,),
        in_specs=[hspec, cspec, cspec],
        out_specs=[hspec, pl.BlockSpec((None, NSA_KV, n_sel), lambda b: (b, 0, 0))],
        out_shape=[jax.ShapeDtypeStruct((B, NSA_HEADS, NSA_HD), F32),
                   jax.ShapeDtypeStruct((B, NSA_KV, n_sel), jnp.int32)],
        compiler_params=_cparams("parallel"),
        name="nsa_sample_select",
    )(q, kc, vc)


N_PAST_SEL = N_SEL - 1
SEL_K_COL = 2 * CMP_W
SEL_V_COL = 3 * CMP_W
BLOCKS_PER_PAGE = PAGE_SIZE // SEL_BLOCK


def _sel_copies(pt_ref, idx_ref, cache_ref, kbuf, vbuf, sem, layer, n_pages, b, slot):
    out = []
    for g in range(NSA_KV):
        for t in range(N_PAST_SEL):
            blk = idx_ref[(b * NSA_KV + g) * N_SEL + t]
            pg = pt_ref[b * n_pages + blk // BLOCKS_PER_PAGE]
            rows = pl.ds((blk % BLOCKS_PER_PAGE) * SEL_BLOCK, SEL_BLOCK)
            dst = pl.ds(t * SEL_BLOCK, SEL_BLOCK)
            plane = (g // GROUPS_PER_PLANE) * LANES
            out.append(pltpu.make_async_copy(cache_ref.at[layer, pg, rows, pl.ds(SEL_K_COL + plane, LANES)],
                                             kbuf.at[slot, g, dst, :], sem.at[slot]))
            out.append(pltpu.make_async_copy(cache_ref.at[layer, pg, rows, pl.ds(SEL_V_COL + plane, LANES)],
                                             vbuf.at[slot, g, dst, :], sem.at[slot]))
    return out


def _softmax_with_self(s, s_self, mask=None):
    if mask is not None:
        s = jnp.where(mask, s, NEG)
    m = jnp.maximum(jnp.max(s, axis=-1, keepdims=True), s_self)
    e = jnp.exp(s - m)
    if mask is not None:
        e = e * mask.astype(F32)
    e_self = jnp.exp(s_self - m)
    return e, e_self, jnp.sum(e, axis=-1, keepdims=True) + e_self


def _nsa_sample_attend_kernel(layer, n_pages, win_first, pt_ref, idx_ref, cache_ref, qr_ref, new_ref, win_ref, oc_ref,
                              gate_ref, o_ref, kbuf, vbuf, sem):
    b = pl.program_id(0)
    nb = pl.num_programs(0)
    slot = b % 2
    mk = functools.partial(_sel_copies, pt_ref, idx_ref, cache_ref, kbuf, vbuf, sem, layer, n_pages)

    @pl.when(b == 0)
    def _():
        for cp in mk(b, slot):
            cp.start()

    @pl.when(b + 1 < nb)
    def _():
        for cp in mk(b + 1, 1 - slot):
            cp.start()

    for cp in mk(b, slot):
        cp.wait()

    qr = qr_ref[...]
    qf = qr.astype(F32)
    head_group = lax.broadcasted_iota(jnp.int32, (NSA_HEADS, 1), 0) // NSA_REP
    gates = _sigmoid(gate_ref[...])
    n_win = win_ref.shape[0]
    wmask = lax.broadcasted_iota(jnp.int32, (NSA_HEADS, n_win), 1) >= win_first
    out = jnp.zeros((NSA_HEADS, NSA_HD), F32)
    for g in range(NSA_KV):
        cols = slice(g * NSA_HD, (g + 1) * NSA_HD)
        ks_new, vs_new = new_ref[0:1, cols], new_ref[1:2, cols]
        kw_new, vw_new = new_ref[2:3, cols], new_ref[3:4, cols]
        half = slice((g % GROUPS_PER_PLANE) * NSA_HD, (g % GROUPS_PER_PLANE + 1) * NSA_HD)
        s = lax.dot_general(qr, kbuf[slot, g][:, half].astype(BF16), NT_DIMS, preferred_element_type=F32)
        e, e_self, d = _softmax_with_self(s, jnp.sum(qf * ks_new, axis=-1, keepdims=True))
        o_s = (jnp.dot(e.astype(BF16), vbuf[slot, g][:, half].astype(BF16), preferred_element_type=F32)
               + e_self * vs_new) / d
        s = lax.dot_general(qr, win_ref[:, cols].astype(BF16), NT_DIMS, preferred_element_type=F32)
        e, e_self, d = _softmax_with_self(s, jnp.sum(qf * kw_new, axis=-1, keepdims=True), wmask)
        vw = win_ref[:, CMP_W + g * NSA_HD:CMP_W + (g + 1) * NSA_HD].astype(BF16)
        o_w = (jnp.dot(e.astype(BF16), vw, preferred_element_type=F32) + e_self * vw_new) / d
        o = gates[:, 0:1] * oc_ref[...] + gates[:, 1:2] * o_s + gates[:, 2:3] * o_w
        out = jnp.where(head_group == g, o, out)
    o_ref[...] = out


def nsa_sample_attend(cache, layer, page_table, idx, qr, new_rows, win, o_cmp, gates, win_first):
    B, n_pages = page_table.shape
    n_win = win.shape[2]
    hspec = lambda: pl.BlockSpec((None, NSA_HEADS, NSA_HD), lambda b, pt, ix: (b, 0, 0))
    return pl.pallas_call(
        functools.partial(_nsa_sample_attend_kernel, layer, n_pages, win_first),
        grid_spec=pltpu.PrefetchScalarGridSpec(
            num_scalar_prefetch=2,
            grid=(B,),
            in_specs=[pl.BlockSpec(memory_space=pl.ANY),
                      hspec(),
                      pl.BlockSpec((None, 4, CMP_W), lambda b, pt, ix: (b, 0, 0)),
                      pl.BlockSpec((None, None, n_win, 2 * CMP_W), lambda b, pt, ix: (layer, b, 0, 0)),
                      hspec(),
                      pl.BlockSpec((None, NSA_HEADS, LANES), lambda b, pt, ix: (b, 0, 0))],
            out_specs=hspec(),
            scratch_shapes=[pltpu.VMEM((2, NSA_KV, N_PAST_SEL * SEL_BLOCK, LANES), F32),
                            pltpu.VMEM((2, NSA_KV, N_PAST_SEL * SEL_BLOCK, LANES), F32),
                            pltpu.SemaphoreType.DMA((2,))]),
        out_shape=jax.ShapeDtypeStruct((B, NSA_HEADS, NSA_HD), F32),
        compiler_params=_cparams("arbitrary"),
        name="nsa_sample_attend",
    )(page_table.reshape(-1), idx.reshape(-1), cache, qr, new_rows, win, o_cmp, gates)


def _merge_kernel(oa_ref, ob_ref, wa_ref, wb_ref, ma_ref, mb_ref, o_ref, wa_bf, wb_bf):
    @pl.when(pl.program_id(1) == 0)
    def _():
        wa_bf[...] = wa_ref[...].astype(BF16)
        wb_bf[...] = wb_ref[...].astype(BF16)

    ya = jnp.dot(oa_ref[...], wa_bf[...], preferred_element_type=F32)
    yb = jnp.dot(ob_ref[...], wb_bf[...], preferred_element_type=F32)
    o_ref[...] = (_sigmoid(ma_ref[...]) * ya + _sigmoid(mb_ref[...]) * yb).astype(o_ref.dtype)


def merge(o_a, o_b, w_a, w_b, zm, layer, *, tm, tn=512):
    M, K = o_a.shape
    D = w_a.shape[-1]
    xspec = pl.BlockSpec((tm, K), lambda j, i: (i, 0))
    wspec = pl.BlockSpec((None, K, tn), lambda j, i: (layer, 0, j))
    return pl.pallas_call(
        _merge_kernel,
        grid=(D // tn, M // tm),
        in_specs=[xspec, xspec, wspec, wspec,
                  pl.BlockSpec((tm, tn), lambda j, i: (i, j)),
                  pl.BlockSpec((tm, tn), lambda j, i: (i, j + D // tn))],
        out_specs=pl.BlockSpec((tm, tn), lambda j, i: (i, j)),
        out_shape=jax.ShapeDtypeStruct((M, D), BF16),
        scratch_shapes=[pltpu.VMEM((K, tn), BF16), pltpu.VMEM((K, tn), BF16)],
        compiler_params=_cparams("arbitrary", "arbitrary"),
        name="merge",
    )(o_a, o_b, w_a, w_b, zm, zm)


MOE_TILE = 256
ROUTE_TILE = 256


def _first_argmax(vals):
    best, idx = vals[0], jnp.zeros(vals[0].shape, F32)
    for e in range(1, len(vals)):
        better = vals[e] > best
        idx = jnp.where(better, float(e), idx)
        best = jnp.where(better, vals[e], best)
    return best, idx


def _router_kernel(x_ref, w_ref, b_ref, o_ref):
    xs = _split3(x_ref[...])
    ws = _split3(w_ref[...])
    terms = ((0, 0), (0, 1), (1, 0), (0, 2), (2, 0), (1, 1))
    logits = b_ref[...] + sum(lax.dot_general(ws[a], xs[c], NT_DIMS, preferred_element_type=F32) for a, c in terms)
    m = jnp.max(logits, axis=0, keepdims=True)
    e = jnp.exp(logits - m)
    probs = e / jnp.sum(e, axis=0, keepdims=True)
    p = [probs[i:i + 1] for i in range(N_EXPERTS)]
    gscore = []
    for g in range(N_GROUPS):
        mem = p[g * EXPERTS_PER_GROUP:(g + 1) * EXPERTS_PER_GROUP]
        pairs = [mem[a] + mem[c] for a in range(EXPERTS_PER_GROUP) for c in range(a + 1, EXPERTS_PER_GROUP)]
        gscore.append(functools.reduce(jnp.maximum, pairs))
    _, g_sel = _first_argmax(gscore)
    masked = [jnp.where(g_sel == float(i // EXPERTS_PER_GROUP), p[i], -1.0) for i in range(N_EXPERTS)]
    v1, i1 = _first_argmax(masked)
    masked2 = [jnp.where(i1 == float(i), -2.0, masked[i]) for i in range(N_EXPERTS)]
    v2, i2 = _first_argmax(masked2)
    tot = v1 + v2
    o_ref[...] = jnp.concatenate([i1, i2, v1 / tot, v2 / tot, jnp.zeros((4,) + i1.shape[1:], F32)], axis=0)


def moe_route(x, w_router, b_router):
    T, D = x.shape
    return pl.pallas_call(
        _router_kernel,
        grid=(T // ROUTE_TILE,),
        in_specs=[pl.BlockSpec((ROUTE_TILE, D), lambda i: (i, 0)),
                  pl.BlockSpec((N_EXPERTS, D), lambda i: (0, 0)),
                  pl.BlockSpec((N_EXPERTS, 1), lambda i: (0, 0))],
        out_specs=pl.BlockSpec((8, ROUTE_TILE), lambda i: (0, i)),
        out_shape=jax.ShapeDtypeStruct((8, T), F32),
        compiler_params=_cparams("parallel"),
        name="moe_route",
    )(x, w_router.T, b_router.reshape(N_EXPERTS, 1))


SUBLANES = 8


def pack_token_rows(xb):
    T, D = xb.shape
    u = lax.bitcast_convert_type(xb, jnp.uint16).astype(jnp.uint32)
    return (u[:, :D // 2] | (u[:, D // 2:] << 16)).reshape(T, D // 2 // LANES, LANES)


def _expert_row_copies(start, hbm, rows_ref, base, buf, sub, sem, to_hbm):
    def body(j, carry):
        idx = rows_ref[base + j]

        @pl.when(idx >= 0)
        def _():
            win = buf.at[pl.ds(pl.multiple_of(j * sub, sub), sub), :]
            row = hbm.at[idx]
            cp = pltpu.make_async_copy(win, row, sem) if to_hbm else pltpu.make_async_copy(row, win, sem)
            if start:
                cp.start()
            else:
                cp.wait()

        return carry

    lax.fori_loop(0, MOE_TILE, body, 0, unroll=8)


def _expert_ffn_kernel(src_ref, dst_ref, te_ref, nt_ref, xw_hbm, w1_ref, w3_ref, w2_ref, y_hbm,
                       xbuf, x2d, y2d, obuf, gsem, ssem):
    t = pl.program_id(0)
    nt = nt_ref[0]
    slot = t % 2
    xsub = xw_hbm.shape[1]
    ysub = y_hbm.shape[1]
    half = w1_ref.shape[0] // 2
    gather = lambda start, tile, sl: _expert_row_copies(start, xw_hbm, src_ref, tile * MOE_TILE, xbuf.at[sl], xsub,
                                                        gsem.at[sl], False)
    scatter = lambda start, tile: _expert_row_copies(start, y_hbm, dst_ref, tile * MOE_TILE, obuf, ysub, ssem, True)

    @pl.when((t == 0) & (nt > 0))
    def _():
        gather(True, t, slot)

    @pl.when(t + 1 < nt)
    def _():
        gather(True, t + 1, 1 - slot)

    @pl.when(t < nt)
    def _():
        gather(False, t, slot)
        xs = xbuf.at[slot]
        for r in range(MOE_TILE // SUBLANES):
            for c in range(xsub):
                x2d[r * SUBLANES:(r + 1) * SUBLANES, c * LANES:(c + 1) * LANES] = (
                    xs[pl.ds(r * SUBLANES * xsub + c, SUBLANES, stride=xsub), :])
        w = x2d[...]
        x_lo = lax.bitcast_convert_type(w << 16, F32).astype(BF16)
        x_hi = lax.bitcast_convert_type(w & jnp.uint32(0xFFFF0000), F32).astype(BF16)
        up = lambda w_ref: (jnp.dot(x_lo, w_ref[:half], preferred_element_type=F32)
                            + jnp.dot(x_hi, w_ref[half:], preferred_element_type=F32))
        h = _silu(up(w1_ref)) * up(w3_ref)
        y2d[...] = jnp.dot(h.astype(BF16), w2_ref[...], preferred_element_type=F32)

        @pl.when(t > 0)
        def _():
            scatter(False, t - 1)

        for r in range(MOE_TILE // SUBLANES):
            for c in range(ysub):
                obuf[pl.ds(r * SUBLANES * ysub + c, SUBLANES, stride=ysub), :] = (
                    y2d[r * SUBLANES:(r + 1) * SUBLANES, c * LANES:(c + 1) * LANES])
        scatter(True, t)

        @pl.when(t == nt - 1)
        def _():
            scatter(False, t)


def expert_ffn(xw, src_token, dst_row, n_rows_out, tile_expert, n_tiles_used, w1, w3, w2, layer):
    P = src_token.shape[0]
    xsub = xw.shape[1]
    D, F = w1.shape[2], w1.shape[3]
    ysub = D // LANES
    wspec = lambda a, c: pl.BlockSpec((None, None, a, c), lambda t, src, dst, te, nt: (layer, te[t], 0, 0))
    return pl.pallas_call(
        _expert_ffn_kernel,
        grid_spec=pltpu.PrefetchScalarGridSpec(
            num_scalar_prefetch=4,
            grid=(P // MOE_TILE,),
            in_specs=[pl.BlockSpec(memory_space=pl.ANY), wspec(D, F), wspec(D, F), wspec(F, D)],
            out_specs=pl.BlockSpec(memory_space=pl.ANY),
            scratch_shapes=[pltpu.VMEM((2, MOE_TILE * xsub, LANES), jnp.uint32),
                            pltpu.VMEM((MOE_TILE, xsub * LANES), jnp.uint32),
                            pltpu.VMEM((MOE_TILE, D), F32),
                            pltpu.VMEM((MOE_TILE * ysub, LANES), F32),
                            pltpu.SemaphoreType.DMA((2,)),
                            pltpu.SemaphoreType.DMA(())]),
        out_shape=jax.ShapeDtypeStruct((n_rows_out, ysub, LANES), F32),
        compiler_params=_cparams("arbitrary"),
        name="expert_ffn",
    )(src_token, dst_row, tile_expert, n_tiles_used, xw, w1, w3, w2)


def _combine_ln_kernel(alpha, x_ref, y1_ref, y2_ref, wt_ref, g_ref, b_ref, o_ref, ob_ref):
    wt = wt_ref[...]
    y = wt[:, 0:1] * y1_ref[...] + wt[:, 1:2] * y2_ref[...]
    v = alpha * x_ref[...] + y
    mu = jnp.mean(v, axis=-1, keepdims=True)
    d = v - mu
    var = jnp.mean(d * d, axis=-1, keepdims=True)
    out = d * lax.rsqrt(var + LN_EPS) * g_ref[...] + b_ref[...]
    o_ref[...] = out
    ob_ref[...] = out.astype(BF16)


def moe_combine_layer_norm(x, ysel, wts, g, b, alpha, *, tm=256):
    T, D = x.shape
    nt = T // tm
    spec = pl.BlockSpec((tm, D), lambda i: (i, 0))
    vec = pl.BlockSpec((1, D), lambda i: (0, 0))
    return pl.pallas_call(
        functools.partial(_combine_ln_kernel, alpha),
        grid=(nt,),
        in_specs=[spec, spec, pl.BlockSpec((tm, D), lambda i: (i + nt, 0)),
                  pl.BlockSpec((tm, 2), lambda i: (i, 0)), vec, vec],
        out_specs=[spec, spec],
        out_shape=[jax.ShapeDtypeStruct((T, D), F32), jax.ShapeDtypeStruct((T, D), BF16)],
        compiler_params=_cparams("parallel"),
        name="moe_combine_ln",
    )(x, ysel, ysel, wts, g.reshape(1, D), b.reshape(1, D))


def moe_dispatch_plan(e1, e2, n_tiles):
    T = e1.shape[0]
    ea = jnp.concatenate([e1, e2])
    onehot = (ea[:, None] == jnp.arange(N_EXPERTS)[None, :]).astype(jnp.int32)
    rank = jnp.take_along_axis(jnp.cumsum(onehot, axis=0) - onehot, ea[:, None], axis=1)[:, 0]
    counts = onehot.sum(0)
    tiles = (counts + MOE_TILE - 1) // MOE_TILE
    tile_end = jnp.cumsum(tiles)
    dest = ((tile_end - tiles) * MOE_TILE)[ea] + rank
    n_slots = n_tiles * MOE_TILE
    src_token = jnp.zeros((n_slots,), jnp.int32).at[dest].set(jnp.tile(jnp.arange(T, dtype=jnp.int32), 2))
    dst_row = jnp.full((n_slots,), -1, jnp.int32).at[dest].set(jnp.arange(2 * T, dtype=jnp.int32))
    tile_expert = jnp.minimum((tile_end[None, :] <= jnp.arange(n_tiles)[:, None]).sum(-1), N_EXPERTS - 1)
    return src_token, dst_row, tile_expert.astype(jnp.int32), tile_end[-1:].astype(jnp.int32)


def moe_block(x, xb, w_router, b_router, w1, w3, w2, layer, g, b, alpha):
    T, D = x.shape
    r = moe_route(x, w_router, b_router)
    e1, e2 = r[0].astype(jnp.int32), r[1].astype(jnp.int32)
    n_tiles = 2 * T // MOE_TILE + N_EXPERTS
    src_token, dst_row, tile_expert, n_used = moe_dispatch_plan(e1, e2, n_tiles)
    y = expert_ffn(pack_token_rows(xb), src_token, dst_row, 2 * T, tile_expert, n_used, w1, w3, w2, layer)
    ysel = y.reshape(2 * T, D)
    return moe_combine_layer_norm(x, ysel, r[2:4].T, g, b, alpha)


def _hgrn_sample_kernel(qt_ref, ft_ref, lbt_ref, onemt_ref, ia_ref, ga_ref, gn_ref, s0_ref, o_ref, s_ref):
    qa = qt_ref[...]
    fa = ft_ref[...]
    q_all = _silu(qa)
    sig = _sigmoid(fa)
    f_all = lbt_ref[...] + onemt_ref[...] * sig
    k_all = onemt_ref[...] * _sigmoid(-fa)
    for h in range(A_HEADS):
        cols = slice(h * A_DK, (h + 1) * A_DK)
        v = ia_ref[:, cols]
        s_new = f_all[:, h:h + 1] * s0_ref[h] + k_all[:, h:h + 1] * v
        s_ref[h] = s_new
        o = jnp.sum(s_new * q_all[:, h:h + 1], axis=0, keepdims=True)
        o_ref[:, cols] = _hgrn_gate_norm(o, ga_ref[:, cols], gn_ref[:, cols]).astype(o_ref.dtype)


def hgrn_sample(qa_t, fa_t, lb_t, onem_t, ia, ga, gnorm, state, layer):
    B = qa_t.shape[0]
    W = A_HEADS * A_DK
    col = pl.BlockSpec((None, A_DK, A_HEADS), lambda b: (b, 0, 0))
    par = pl.BlockSpec((A_DK, A_HEADS), lambda b: (0, 0))
    row = pl.BlockSpec((None, 1, W), lambda b: (b, 0, 0))
    return pl.pallas_call(
        _hgrn_sample_kernel,
        grid=(B,),
        in_specs=[col, col, par, par, row, row, pl.BlockSpec((1, W), lambda b: (0, 0)),
                  pl.BlockSpec((None, None, A_HEADS, A_DK, A_DK), lambda b: (layer, b, 0, 0, 0))],
        out_specs=[row, pl.BlockSpec((None, A_HEADS, A_DK, A_DK), lambda b: (b, 0, 0, 0))],
        out_shape=[jax.ShapeDtypeStruct((B, 1, W), BF16), jax.ShapeDtypeStruct((B, A_HEADS, A_DK, A_DK), F32)],
        compiler_params=_cparams("parallel"),
        name="hgrn_sample",
    )(qa_t, fa_t, lb_t, onem_t, ia, ga, gnorm, state)


MM_TILE_M = 768
MM_TILE_N = 512
W_A = A_HEADS * A_DK
W_Q = NSA_HEADS * NSA_HD
COL_QB = 4 * W_A
COL_KV = COL_QB + W_Q
COL_GB = COL_KV + 6 * CMP_W
COL_MA = COL_GB + 3 * NSA_HEADS


def _pad_lanes(a):
    return jnp.pad(a, [(0, 0)] * (a.ndim - 1) + [(0, LANES - a.shape[-1])])


def kernel(x_prompt, x_sample, cache_nsa_kv, cache_nsa_win, state_hgrn, page_table, w_in, lb_raw, gnorm_a, cmp_pe, cmp_w1, cmp_w2, w_branch_a, w_branch_b, w_out, ln1_g, ln1_b, ln2_g, ln2_b, w_router, b_router, w_e1, w_e3, w_e2):
    depth, D = w_in.shape[0], w_in.shape[1]
    L, B = x_prompt.shape[1], x_sample.shape[0]
    assert x_prompt.shape[0] == 1 and x_sample.shape[1] == 1
    n_pool = cache_nsa_kv.shape[1]
    past_len = page_table.shape[1] * PAGE_SIZE
    win_rows = cache_nsa_win.shape[2]
    T = L + B
    M = -(-T // MM_TILE_M) * MM_TILE_M
    alpha = (2 * depth) ** 0.25
    prompt, sample = slice(0, L), slice(L, T)

    lbs = jnp.cumsum(jax.nn.softmax(lb_raw.astype(F32), axis=0), axis=0)
    lbs = lbs - lbs[0:1]
    pos = jnp.concatenate([jnp.arange(L), jnp.full((B,), past_len), jnp.zeros((M - T,), jnp.int32)])
    cos, sin = rope_tables(pos)
    ones, zeros = jnp.ones_like(cos), jnp.zeros_like(cos)
    cache = cache_nsa_kv.reshape(depth, n_pool, PAGE_SIZE, 4 * CMP_W)
    win_cache = cache_nsa_win.reshape(depth, B, win_rows, 2 * CMP_W)
    w1b, w3b, w2b = w_e1.astype(BF16), w_e3.astype(BF16), w_e2.astype(BF16)
    scale = NSA_HD ** -0.5

    x = jnp.concatenate([x_prompt[0], x_sample[:, 0], jnp.zeros((M - T, D), F32)])
    xb = x.astype(BF16)
    heads_major = lambda a, h: a.reshape(a.shape[0], h, NSA_HD).transpose(1, 0, 2)
    col_major = lambda a: a.reshape(B, A_HEADS, A_DK).transpose(0, 2, 1)
    outs = [[] for _ in range(6)]
    for l in range(depth):
        lb = lbs[l]
        mm = functools.partial(matmul, tm=MM_TILE_M, tn=MM_TILE_N)
        z = mm(xb, w_in, l, col0=0, ncols=COL_GB)
        zg = matmul(xb, _pad_lanes(w_in[l, :, COL_GB:COL_MA])[None], 0, tm=MM_TILE_M, tn=LANES)
        zm = mm(xb, w_in[l, :, COL_MA:][None], 0)

        o_a_p, st_p = hgrn_prompt(z, L, jnp.log(lb)[None], jnp.log1p(-lb)[None], (1.0 - lb)[None], gnorm_a[l][None])
        zs = z[sample]
        o_a_s, st_s = hgrn_sample(col_major(zs[:, :W_A]), col_major(zs[:, W_A:2 * W_A]),
                                  lb.reshape(A_HEADS, A_DK).T, (1.0 - lb).reshape(A_HEADS, A_DK).T,
                                  zs[:, None, 2 * W_A:3 * W_A], zs[:, None, 3 * W_A:4 * W_A], gnorm_a[l][None],
                                  state_hgrn, l)
        o_a = jnp.concatenate([o_a_p, o_a_s[:, 0], jnp.zeros((M - T, W_A), BF16)])

        q_plain = rope(z, COL_QB, W_Q, ones, zeros, scale=scale, out_dtype=BF16)
        q_rot = rope(z, COL_QB, W_Q, cos, sin, scale=scale, out_dtype=BF16)
        ks_rot = rope(z, COL_KV + 2 * CMP_W, CMP_W, cos, sin)
        kw_rot = rope(z, COL_KV + 4 * CMP_W, CMP_W, cos, sin)
        v_sel = z[:, COL_KV + 3 * CMP_W:COL_KV + 4 * CMP_W]
        v_win = z[:, COL_KV + 5 * CMP_W:COL_KV + 6 * CMP_W]
        rows = jnp.concatenate([z[:, COL_KV:COL_KV + 2 * CMP_W], ks_rot, v_sel], axis=-1)
        win_new = jnp.concatenate([kw_rot, v_win], axis=-1)
        cw = compress_weights(cmp_pe[l], cmp_w1[l], cmp_w2[l])

        kc_p, vc_p = compress_paged(rows[prompt].reshape(1, L // PAGE_SIZE, PAGE_SIZE, 4 * CMP_W), 0,
                                    jnp.arange(L // PAGE_SIZE, dtype=jnp.int32)[None], *cw)
        grp = lambda a: heads_major(a, NSA_KV).astype(BF16)
        gates_p = _pad_lanes(zg[prompt, :3 * NSA_HEADS].reshape(L, NSA_KV, 3 * NSA_REP).transpose(1, 0, 2))
        o_b_p = nsa_prompt(heads_major(q_plain[prompt], NSA_HEADS), heads_major(q_rot[prompt], NSA_HEADS),
                           grp(kc_p[0]), grp(vc_p[0]), grp(ks_rot[prompt]), grp(v_sel[prompt]),
                           grp(kw_rot[prompt]), grp(v_win[prompt]), gates_p)

        kc_s, vc_s = compress_paged(cache, l, page_table, *cw)
        o_c_s, idx = nsa_sample_select(q_plain[sample].reshape(B, NSA_HEADS, NSA_HD), kc_s, vc_s,
                                       past_len, past_len // SEL_BLOCK)
        rs, ws = rows[sample], win_new[sample]
        new_rows = jnp.stack([rs[:, 2 * CMP_W:3 * CMP_W], rs[:, 3 * CMP_W:], ws[:, :CMP_W], ws[:, CMP_W:]], axis=1)
        gates_s = _pad_lanes(zg[sample, :3 * NSA_HEADS].reshape(B, NSA_HEADS, 3))
        o_b_s = nsa_sample_attend(cache, l, page_table, idx[:, :, :N_SEL], q_rot[sample].reshape(B, NSA_HEADS, NSA_HD),
                                  new_rows, win_cache, o_c_s, gates_s, win_rows - WINDOW + 1)
        o_b = jnp.concatenate([o_b_p, o_b_s.reshape(B, W_Q).astype(BF16), jnp.zeros((M - T, W_Q), BF16)])

        h = merge(o_a, o_b, w_branch_a, w_branch_b, zm, l, tm=MM_TILE_M)
        y = mm(h, w_out, l)
        x1, x1b = residual_layer_norm(x, y, ln1_g[l], ln1_b[l], alpha)
        x, xb = moe_block(x1, x1b, w_router, b_router, w1b, w3b, w2b, l, ln2_g[l], ln2_b[l], alpha)

        wbp = min(WINDOW, L)
        outs[0].append(rows[prompt].reshape(1, L, 4, NSA_KV, NSA_HD))
        outs[1].append(rs.reshape(B, 1, 4, NSA_KV, NSA_HD))
        outs[2].append(win_new[L - wbp:L].reshape(1, wbp, 2, NSA_KV, NSA_HD))
        outs[3].append(jnp.concatenate([cache_nsa_win[l][:, 1:], ws.reshape(B, 1, 2, NSA_KV, NSA_HD)], axis=1))
        outs[4].append(st_p[None])
        outs[5].append(st_s)
    return (x[prompt][None], x[sample][:, None]) + tuple(jnp.stack(o) for o in outs)
```

```python
import functools
import math

import jax
import jax.numpy as jnp
import numpy as np
from jax import lax
from jax.experimental import pallas as pl
from jax.experimental.pallas import tpu as pltpu

F32 = jnp.float32
BF16 = jnp.bfloat16

A_HEADS = 8
A_DK = 128
NSA_HEADS = 16
NSA_KV = 4
NSA_REP = NSA_HEADS // NSA_KV
NSA_HD = 64
CMP_BLOCK = 32
CMP_STRIDE = 16
CMP_HIDDEN = 128
SEL_BLOCK = 64
N_SEL = 16
WINDOW = 512
PAGE_SIZE = 128
ROPE_THETA = 10000.0
N_EXPERTS = 16
N_GROUPS = 4
EXPERTS_PER_GROUP = N_EXPERTS // N_GROUPS
LN_EPS = 1e-5
NEG = -1e30

VMEM_LIMIT_BYTES = 56 * 1024 * 1024
LANES = 128


def _cparams(*sem):
    return pltpu.CompilerParams(dimension_semantics=sem, vmem_limit_bytes=VMEM_LIMIT_BYTES)


def _sigmoid(x):
    return 1.0 / (1.0 + jnp.exp(-x))


def _silu(x):
    return x * _sigmoid(x)


def _mm_kernel(x_ref, w_ref, o_ref, wb_ref):
    @pl.when(pl.program_id(1) == 0)
    def _():
        wb_ref[...] = w_ref[...].astype(BF16)

    o_ref[...] = jnp.dot(x_ref[...], wb_ref[...], preferred_element_type=F32).astype(o_ref.dtype)


def matmul(x, w, layer, *, tm, tn, col0=0, ncols=None, out_dtype=F32):
    M, K = x.shape
    ncols = w.shape[-1] if ncols is None else ncols
    assert M % tm == 0 and ncols % tn == 0 and col0 % tn == 0
    off = col0 // tn
    return pl.pallas_call(
        _mm_kernel,
        grid=(ncols // tn, M // tm),
        in_specs=[pl.BlockSpec((tm, K), lambda j, i: (i, 0)),
                  pl.BlockSpec((None, K, tn), lambda j, i: (layer, 0, j + off))],
        out_specs=pl.BlockSpec((tm, tn), lambda j, i: (i, j)),
        out_shape=jax.ShapeDtypeStruct((M, ncols), out_dtype),
        scratch_shapes=[pltpu.VMEM((K, tn), BF16)],
        compiler_params=_cparams("arbitrary", "arbitrary"),
        name="matmul",
    )(x, w)


def _ln_kernel(alpha, x_ref, y_ref, g_ref, b_ref, o_ref, ob_ref):
    v = alpha * x_ref[...] + y_ref[...]
    mu = jnp.mean(v, axis=-1, keepdims=True)
    d = v - mu
    var = jnp.mean(d * d, axis=-1, keepdims=True)
    out = d * lax.rsqrt(var + LN_EPS) * g_ref[...] + b_ref[...]
    o_ref[...] = out
    ob_ref[...] = out.astype(BF16)


def residual_layer_norm(x, y, g, b, alpha, *, tm=256):
    M, D = x.shape
    spec = pl.BlockSpec((tm, D), lambda i: (i, 0))
    vec = pl.BlockSpec((1, D), lambda i: (0, 0))
    return pl.pallas_call(
        functools.partial(_ln_kernel, alpha),
        grid=(M // tm,),
        in_specs=[spec, spec, vec, vec],
        out_specs=[spec, spec],
        out_shape=[jax.ShapeDtypeStruct((M, D), F32), jax.ShapeDtypeStruct((M, D), BF16)],
        compiler_params=_cparams("parallel"),
        name="residual_ln",
    )(x, y, g.reshape(1, D), b.reshape(1, D))


def _rope_kernel(scale, x_ref, cos_ref, sin_ref, o_ref):
    cos = cos_ref[...]
    sin = sin_ref[...]
    lane = lax.broadcasted_iota(jnp.int32, cos.shape, 1)
    first_half = (lane % NSA_HD) < (NSA_HD // 2)
    for j in range(x_ref.shape[1] // LANES):
        x = x_ref[:, j * LANES:(j + 1) * LANES]
        partner = jnp.where(first_half, pltpu.roll(x, LANES - NSA_HD // 2, 1), pltpu.roll(x, NSA_HD // 2, 1))
        o_ref[:, j * LANES:(j + 1) * LANES] = ((x * cos + partner * sin) * scale).astype(o_ref.dtype)


def rope(z, col0, width, cos, sin, *, scale=1.0, out_dtype=F32, tm=256):
    M = z.shape[0]
    assert col0 % width == 0 and width % LANES == 0
    return pl.pallas_call(
        functools.partial(_rope_kernel, scale),
        grid=(M // tm,),
        in_specs=[pl.BlockSpec((tm, width), lambda i: (i, col0 // width)),
                  pl.BlockSpec((tm, LANES), lambda i: (i, 0)),
                  pl.BlockSpec((tm, LANES), lambda i: (i, 0))],
        out_specs=pl.BlockSpec((tm, width), lambda i: (i, 0)),
        out_shape=jax.ShapeDtypeStruct((M, width), out_dtype),
        compiler_params=_cparams("parallel"),
        name="rope",
    )(z, cos, sin)


def rope_tables(pos):
    half = NSA_HD // 2
    inv = jnp.power(ROPE_THETA, -jnp.arange(half, dtype=F32) / half)
    ang = pos.astype(F32)[:, None] * inv[None, :]
    cos, sin = jnp.cos(ang), jnp.sin(ang)
    reps = LANES // NSA_HD
    return jnp.tile(jnp.concatenate([cos, cos], -1), (1, reps)), jnp.tile(jnp.concatenate([-sin, sin], -1), (1, reps))


HG_CHUNK = 64
HG_SUB = 16
HG_BLOCK = 256


def _split3(x):
    hi = x.astype(BF16)
    r1 = x - hi.astype(F32)
    mid = r1.astype(BF16)
    lo = (r1 - mid.astype(F32)).astype(BF16)
    return hi, mid, lo


def _log_forget(fa, loglb, log1mlb):
    log_sig = jnp.minimum(fa, 0.0) - jnp.log(1.0 + jnp.exp(-jnp.abs(fa)))
    b = log1mlb + log_sig
    mx = jnp.maximum(loglb, b)
    return mx + jnp.log(1.0 + jnp.exp(-jnp.abs(loglb - b)))


def _hgrn_gate_norm(o, ga, gn):
    o = o * lax.rsqrt(jnp.mean(o * o, axis=-1, keepdims=True) + 1e-6)
    return o * gn * _silu(ga)


def _hgrn_prompt_kernel(qa_ref, fa_ref, ia_ref, ga_ref, loglb_ref, log1mlb_ref, onemlb_ref, gn_ref,
                        o_ref, s_ref, st_ref):
    c = pl.program_id(1)

    @pl.when(c == 0)
    def _():
        st_ref[...] = jnp.zeros_like(st_ref)

    loglb, log1mlb, onemlb, gn = loglb_ref[...], log1mlb_ref[...], onemlb_ref[...], gn_ref[...]
    row = lax.broadcasted_iota(jnp.int32, (HG_CHUNK, HG_CHUNK), 0)
    col = lax.broadcasted_iota(jnp.int32, (HG_CHUNK, HG_CHUNK), 1)
    tril = (col <= row).astype(BF16)
    srow = lax.broadcasted_iota(jnp.int32, (HG_SUB, HG_SUB), 0)
    scol = lax.broadcasted_iota(jnp.int32, (HG_SUB, HG_SUB), 1)

    def chunk(ci, carry):
        r0 = pl.multiple_of(ci * HG_CHUNK, HG_CHUNK)
        rows = pl.ds(r0, HG_CHUNK)
        qa, fa, v = qa_ref[rows, :], fa_ref[rows, :], ia_ref[rows, :]
        q = _silu(qa)
        logf = _log_forget(fa, loglb, log1mlb)
        k = onemlb * _sigmoid(-fa)
        b = sum(jnp.dot(tril, t, preferred_element_type=F32) for t in _split3(logf))
        st = st_ref[...]
        vb = v.astype(BF16)
        o_inter = lax.dot_general((q * jnp.exp(b)).astype(BF16), st.astype(BF16), (((1,), (1,)), ((), ())),
                                  preferred_element_type=F32)
        for i in range(HG_CHUNK // HG_SUB):
            lo = i * HG_SUB
            qi, bi, ki = q[lo:lo + HG_SUB], b[lo:lo + HG_SUB], k[lo:lo + HG_SUB]
            o = o_inter[lo:lo + HG_SUB]
            if i > 0:
                anchor = b[lo - 1:lo]
                qt = (qi * jnp.exp(bi - anchor)).astype(BF16)
                kt = (k[:lo] * jnp.exp(anchor - b[:lo])).astype(BF16)
                a_off = lax.dot_general(qt, kt, (((1,), (1,)), ((), ())), preferred_element_type=F32)
                o = o + jnp.dot(a_off.astype(BF16), vb[:lo], preferred_element_type=F32)
            a_d = jnp.zeros((HG_SUB, HG_SUB), F32)
            for s in range(HG_SUB):
                p = qi * ki[s:s + 1] * jnp.exp(jnp.minimum(bi - bi[s:s + 1], 0.0))
                a_d = jnp.where(scol == s, jnp.sum(p, axis=-1, keepdims=True), a_d)
            a_d = jnp.where(scol <= srow, a_d, 0.0)
            o = o + jnp.dot(a_d.astype(BF16), vb[lo:lo + HG_SUB], preferred_element_type=F32)
            ga = ga_ref[pl.ds(r0 + lo, HG_SUB), :]
            o_ref[pl.ds(r0 + lo, HG_SUB), :] = _hgrn_gate_norm(o, ga, gn).astype(o_ref.dtype)
        b_last = b[HG_CHUNK - 1:HG_CHUNK]
        kt = (k * jnp.exp(b_last - b)).astype(BF16)
        st_ref[...] = st * jnp.exp(b_last) + lax.dot_general(vb, kt, (((0,), (0,)), ((), ())),
                                                             preferred_element_type=F32)
        return carry

    lax.fori_loop(0, HG_BLOCK // HG_CHUNK, chunk, 0, unroll=True)

    @pl.when(c == pl.num_programs(1) - 1)
    def _():
        s_ref[...] = st_ref[...].T


def hgrn_prompt(z, L, loglb, log1mlb, onemlb, gnorm):
    blk = lambda part: pl.BlockSpec((HG_BLOCK, A_DK), lambda h, c: (c, part * A_HEADS + h))
    vec = pl.BlockSpec((1, A_DK), lambda h, c: (0, h))
    return pl.pallas_call(
        _hgrn_prompt_kernel,
        grid=(A_HEADS, L // HG_BLOCK),
        in_specs=[blk(0), blk(1), blk(2), blk(3), vec, vec, vec, vec],
        out_specs=[pl.BlockSpec((HG_BLOCK, A_DK), lambda h, c: (c, h)),
                   pl.BlockSpec((None, A_DK, A_DK), lambda h, c: (h, 0, 0))],
        out_shape=[jax.ShapeDtypeStruct((L, A_HEADS * A_DK), BF16),
                   jax.ShapeDtypeStruct((A_HEADS, A_DK, A_DK), F32)],
        scratch_shapes=[pltpu.VMEM((A_DK, A_DK), F32)],
        compiler_params=_cparams("arbitrary", "arbitrary"),
        name="hgrn_prompt",
    )(z, z, z, z, loglb, log1mlb, onemlb, gnorm)


CMP_PAGES = 32
CMP_ROWS = CMP_PAGES * PAGE_SIZE
CMP_OUT = CMP_ROWS // CMP_STRIDE
CMP_W = NSA_KV * NSA_HD
CMP_PLANES = 2 * CMP_W // LANES
GROUPS_PER_PLANE = LANES // NSA_HD


def _compress_copies(pt_ref, cache_ref, xbuf, sem, layer, bb, cc, slot, n_pages):
    out = []
    nxt = jnp.minimum((cc + 1) * CMP_PAGES, n_pages - 1)
    for q in range(CMP_PLANES):
        cols = pl.ds(q * LANES, LANES)
        for j in range(CMP_PAGES):
            pg = pt_ref[bb, cc * CMP_PAGES + j]
            out.append(pltpu.make_async_copy(cache_ref.at[layer, pg, :, cols],
                                             xbuf.at[slot, q, pl.ds(j * PAGE_SIZE, PAGE_SIZE), :], sem.at[slot]))
        out.append(pltpu.make_async_copy(cache_ref.at[layer, pt_ref[bb, nxt], pl.ds(0, CMP_STRIDE), cols],
                                         xbuf.at[slot, q, pl.ds(CMP_ROWS, CMP_STRIDE), :], sem.at[slot]))
    return out


def _compress_kernel(layer, pt_ref, cache_ref, w1_hbm, pe_ref, w2_ref, kc_ref, vc_ref, xbuf, w1buf, xcat, sem, wsem):
    b, c = pl.program_id(0), pl.program_id(1)
    nb, nc = pl.num_programs(0), pl.num_programs(1)
    step = b * nc + c
    slot = step % 2
    n_pages = nc * CMP_PAGES
    mk = functools.partial(_compress_copies, pt_ref, cache_ref, xbuf, sem, layer)

    @pl.when(step == 0)
    def _():
        wcopy = pltpu.make_async_copy(w1_hbm, w1buf, wsem)
        wcopy.start()
        for cp in mk(b, c, slot, n_pages):
            cp.start()
        wcopy.wait()

    @pl.when(step + 1 < nb * nc)
    def _():
        last_c = c == nc - 1
        for cp in mk(jnp.where(last_c, b + 1, b), jnp.where(last_c, 0, c + 1), 1 - slot, n_pages):
            cp.start()

    for cp in mk(b, c, slot, n_pages):
        cp.wait()

    planes_per_out = CMP_PLANES // 2
    for t, o_ref in enumerate((kc_ref, vc_ref)):
        for q in range(planes_per_out):
            xs = xbuf.at[slot, t * planes_per_out + q]
            for p in range(CMP_BLOCK):
                x = xs[pl.ds(p, CMP_OUT, stride=CMP_STRIDE), :] + pe_ref[t, p]
                xcat[:, p * LANES:(p + 1) * LANES] = x.astype(BF16)
            h = jnp.dot(xcat[...], w1buf[t], preferred_element_type=F32)
            o_ref[:, q * LANES:(q + 1) * LANES] = jnp.dot(_silu(h).astype(BF16), w2_ref[t],
                                                          preferred_element_type=F32)


def compress_weights(pe, w1, w2):
    eye = jnp.eye(GROUPS_PER_PLANE, dtype=F32)
    w1r = w1.reshape(2, CMP_BLOCK, NSA_HD, CMP_HIDDEN)
    w1bd = jnp.einsum('gh,tpdn->tpgdhn', eye, w1r).reshape(2, CMP_BLOCK * LANES, GROUPS_PER_PLANE * CMP_HIDDEN)
    w2bd = jnp.einsum('gh,tnd->tgnhd', eye, w2).reshape(2, GROUPS_PER_PLANE * CMP_HIDDEN, LANES)
    pet = jnp.tile(pe, (1, 1, GROUPS_PER_PLANE)).reshape(2, CMP_BLOCK, 1, LANES)
    return w1bd.astype(BF16), pet, w2bd.astype(BF16)


def compress_paged(cache, layer, page_table, w1bd, pet, w2bd):
    B, n_pages = page_table.shape
    assert n_pages % CMP_PAGES == 0
    nc = n_pages // CMP_PAGES
    out = jax.ShapeDtypeStruct((B, nc * CMP_OUT, CMP_W), F32)
    ospec = pl.BlockSpec((None, CMP_OUT, CMP_W), lambda b, c, pt: (b, c, 0))
    return pl.pallas_call(
        functools.partial(_compress_kernel, layer),
        grid_spec=pltpu.PrefetchScalarGridSpec(
            num_scalar_prefetch=1,
            grid=(B, nc),
            in_specs=[pl.BlockSpec(memory_space=pl.ANY),
                      pl.BlockSpec(memory_space=pl.ANY),
                      pl.BlockSpec(pet.shape, lambda b, c, pt: (0, 0, 0, 0)),
                      pl.BlockSpec(w2bd.shape, lambda b, c, pt: (0, 0, 0))],
            out_specs=[ospec, ospec],
            scratch_shapes=[pltpu.VMEM((2, CMP_PLANES, CMP_ROWS + CMP_STRIDE, LANES), F32),
                            pltpu.VMEM(w1bd.shape, BF16),
                            pltpu.VMEM((CMP_OUT, CMP_BLOCK * LANES), BF16),
                            pltpu.SemaphoreType.DMA((2,)),
                            pltpu.SemaphoreType.DMA(())]),
        out_shape=[out, out],
        compiler_params=_cparams("arbitrary", "arbitrary"),
        name="nsa_compress",
    )(page_table, cache, w1bd, pet, w2bd)


NSA_TQ = 256
NSA_TK = 512
SEL_PER_CMP = SEL_BLOCK // CMP_STRIDE
NT_DIMS = (((1,), (1,)), ((), ()))


def _masked_softmax(s, mask):
    s = jnp.where(mask, s, NEG)
    e = jnp.exp(s - jnp.max(s, axis=-1, keepdims=True)) * mask.astype(F32)
    d = jnp.sum(e, axis=-1, keepdims=True)
    return e / jnp.where(d > 0, d, 1.0)


def _pool_matrix(n_cmp, n_sel):
    i = lax.broadcasted_iota(jnp.int32, (n_cmp, n_sel), 0)
    j = lax.broadcasted_iota(jnp.int32, (n_cmp, n_sel), 1)
    lo = SEL_PER_CMP * j - (CMP_BLOCK // CMP_STRIDE - 1)
    return ((i >= lo) & (i <= SEL_PER_CMP * j + SEL_PER_CMP - 1)).astype(BF16)


def _importance(psum, pool):
    return sum(jnp.dot(t, pool, preferred_element_type=F32) for t in _split3(psum))


def _top_k_mask(score, k):
    lane = lax.broadcasted_iota(jnp.int32, score.shape, 1).astype(F32)
    sel = jnp.zeros(score.shape, F32)
    for _ in range(k):
        m = jnp.max(score, axis=-1, keepdims=True)
        idx = jnp.min(jnp.where(score == m, lane, float(score.shape[-1])), axis=-1, keepdims=True)
        hit = lane == idx
        sel = jnp.where(hit, 1.0, sel)
        score = jnp.where(hit, -jnp.inf, score)
    return sel


def _nsa_prompt_kernel(q_ref, qr_ref, kc_ref, vc_ref, ks_ref, vs_ref, kw_ref, vw_ref, gate_ref, o_ref,
                       sel_ref, m_ref, l_ref, acc_ref, oc_ref):
    tq, tk = NSA_TQ, NSA_TK
    s0 = pl.program_id(1) * tq
    tpos = s0 + lax.broadcasted_iota(jnp.int32, (tq, 1), 0)
    n_cmp = kc_ref.shape[0]
    n_sel = sel_ref.shape[1]

    cidx = lax.broadcasted_iota(jnp.int32, (tq, n_cmp), 1)
    cmask = (cidx * CMP_STRIDE + (CMP_BLOCK - 1) <= tpos) & (cidx < n_cmp - 1)
    psum = jnp.zeros((tq, n_cmp), F32)
    for r in range(NSA_REP):
        s = lax.dot_general(q_ref[r], kc_ref[...], NT_DIMS, preferred_element_type=F32)
        p = _masked_softmax(s, cmask)
        oc_ref[r] = jnp.dot(p.astype(BF16), vc_ref[...], preferred_element_type=F32)
        psum = psum + p

    imp = _importance(psum, _pool_matrix(n_cmp, n_sel))
    blk = lax.broadcasted_iota(jnp.int32, (tq, n_sel), 1)
    valid = blk * SEL_BLOCK <= tpos
    forced = (blk == 0) | (blk == tpos // SEL_BLOCK)
    score = jnp.where(forced, jnp.inf, jnp.where(valid, imp, -jnp.inf))
    sel_ref[...] = _top_k_mask(score, N_SEL).astype(BF16)

    m_ref[...] = jnp.full(m_ref.shape, NEG, F32)
    l_ref[...] = jnp.zeros(l_ref.shape, F32)
    acc_ref[...] = jnp.zeros(acc_ref.shape, F32)

    def sweep(kt, carry):
        k0 = pl.multiple_of(kt * tk, tk)
        jj = lax.broadcasted_iota(jnp.int32, (n_sel, tk), 0)
        ss = lax.broadcasted_iota(jnp.int32, (n_sel, tk), 1)
        expand = (jj == kt * (tk // SEL_BLOCK) + ss // SEL_BLOCK).astype(BF16)
        chosen = jnp.dot(sel_ref[...], expand, preferred_element_type=F32)
        kpos = k0 + lax.broadcasted_iota(jnp.int32, (tq, tk), 1)
        allowed = (chosen > 0.5) & (kpos <= tpos)
        ks = ks_ref[pl.ds(k0, tk), :]
        vs = vs_ref[pl.ds(k0, tk), :]
        for r in range(NSA_REP):
            s = lax.dot_general(qr_ref[r], ks, NT_DIMS, preferred_element_type=F32)
            s = jnp.where(allowed, s, NEG)
            m_old = m_ref[r]
            m_new = jnp.maximum(m_old, jnp.max(s, axis=-1, keepdims=True))
            p = jnp.exp(s - m_new)
            alpha = jnp.exp(m_old - m_new)
            l_ref[r] = alpha * l_ref[r] + jnp.sum(p, axis=-1, keepdims=True)
            acc_ref[r] = alpha * acc_ref[r] + jnp.dot(p.astype(BF16), vs, preferred_element_type=F32)
            m_ref[r] = m_new
        return carry

    lax.fori_loop(0, (s0 + tq + tk - 1) // tk, sweep, 0)

    band = WINDOW + tq
    w0 = pl.multiple_of(jnp.maximum(s0 - WINDOW, 0), tq)
    wpos = w0 + lax.broadcasted_iota(jnp.int32, (tq, band), 1)
    wmask = (wpos <= tpos) & (wpos > tpos - WINDOW)
    kw = kw_ref[pl.ds(w0, band), :]
    vw = vw_ref[pl.ds(w0, band), :]
    gates = _sigmoid(gate_ref[...])
    for r in range(NSA_REP):
        s = lax.dot_general(qr_ref[r], kw, NT_DIMS, preferred_element_type=F32)
        o_w = jnp.dot(_masked_softmax(s, wmask).astype(BF16), vw, preferred_element_type=F32)
        o_s = acc_ref[r] / l_ref[r]
        o = (gates[:, 3 * r:3 * r + 1] * oc_ref[r] + gates[:, 3 * r + 1:3 * r + 2] * o_s
             + gates[:, 3 * r + 2:3 * r + 3] * o_w)
        o_ref[:, r * NSA_HD:(r + 1) * NSA_HD] = o.astype(o_ref.dtype)


def nsa_prompt(q, qr, kc, vc, ks, vs, kw, vw, gates):
    L = q.shape[1]
    n_cmp = kc.shape[1]
    n_sel = L // SEL_BLOCK
    assert L % NSA_TK == 0 and L >= WINDOW + NSA_TQ
    qspec = pl.BlockSpec((NSA_REP, NSA_TQ, NSA_HD), lambda g, i: (g, i, 0))
    cspec = pl.BlockSpec((None, n_cmp, NSA_HD), lambda g, i: (g, 0, 0))
    kspec = pl.BlockSpec((None, L, NSA_HD), lambda g, i: (g, 0, 0))
    per_head = lambda w: pltpu.VMEM((NSA_REP, NSA_TQ, w), F32)
    return pl.pallas_call(
        _nsa_prompt_kernel,
        grid=(NSA_KV, L // NSA_TQ),
        in_specs=[qspec, qspec, cspec, cspec, kspec, kspec, kspec, kspec,
                  pl.BlockSpec((None, NSA_TQ, LANES), lambda g, i: (g, i, 0))],
        out_specs=pl.BlockSpec((NSA_TQ, NSA_REP * NSA_HD), lambda g, i: (i, g)),
        out_shape=jax.ShapeDtypeStruct((L, NSA_HEADS * NSA_HD), BF16),
        scratch_shapes=[pltpu.VMEM((NSA_TQ, n_sel), BF16),
                        per_head(1), per_head(1),
                        per_head(NSA_HD), per_head(NSA_HD)],
        compiler_params=_cparams("parallel", "arbitrary"),
        name="nsa_prompt",
    )(q, qr, kc, vc, ks, vs, kw, vw, gates)


def _top_k_indices(score, k):
    lane = lax.broadcasted_iota(jnp.int32, score.shape, 1).astype(F32)
    out = jnp.zeros(score.shape, F32)
    for t in range(k):
        m = jnp.max(score, axis=-1, keepdims=True)
        idx = jnp.min(jnp.where(score == m, lane, float(score.shape[-1])), axis=-1, keepdims=True)
        out = jnp.where(lane == float(t), idx, out)
        score = jnp.where(lane == idx, -jnp.inf, score)
    return out.astype(jnp.int32)


def _nsa_sample_select_kernel(q_pos, q_ref, kc_ref, vc_ref, oc_ref, idx_ref):
    n_cmp = kc_ref.shape[0]
    n_sel = idx_ref.shape[-1]
    q = q_ref[...]
    head_group = lax.broadcasted_iota(jnp.int32, (NSA_HEADS, 1), 0) // NSA_REP
    cidx = lax.broadcasted_iota(jnp.int32, (NSA_HEADS, n_cmp), 1)
    cmask = (cidx * CMP_STRIDE + (CMP_BLOCK - 1) <= q_pos) & (cidx < n_cmp - 1)
    grp_row = lax.broadcasted_iota(jnp.int32, (NSA_KV, 1), 0)
    oc = jnp.zeros((NSA_HEADS, NSA_HD), F32)
    psum = jnp.zeros((NSA_KV, n_cmp), F32)
    for g in range(NSA_KV):
        cols = slice(g * NSA_HD, (g + 1) * NSA_HD)
        s = lax.dot_general(q, kc_ref[:, cols].astype(BF16), NT_DIMS, preferred_element_type=F32)
        p = _masked_softmax(s, cmask)
        o = jnp.dot(p.astype(BF16), vc_ref[:, cols].astype(BF16), preferred_element_type=F32)
        mine = head_group == g
        oc = jnp.where(mine, o, oc)
        pg = jnp.sum(jnp.where(mine, p, 0.0), axis=0, keepdims=True)
        psum = jnp.where(grp_row == g, pg, psum)
    oc_ref[...] = oc
    imp = _importance(psum, _pool_matrix(n_cmp, n_sel))
    blk = lax.broadcasted_iota(jnp.int32, (NSA_KV, n_sel), 1)
    score = jnp.where(blk == 0, jnp.inf, imp)
    idx_ref[...] = _top_k_indices(score, N_SEL - 1)


def nsa_sample_select(q, kc, vc, q_pos, n_sel):
    B, n_cmp = kc.shape[0], kc.shape[1]
    cspec = pl.BlockSpec((None, n_cmp, CMP_W), lambda b: (b, 0, 0))
    hspec = pl.BlockSpec((None, NSA_HEADS, NSA_HD), lambda b: (b, 0, 0))
    return pl.pallas_call(
        functools.partial(_nsa_sample_select_kernel, q_pos),
        grid=(B,),
        in_specs=[hspec, cspec, cspec],
        out_specs=[hspec, pl.BlockSpec((None, NSA_KV, n_sel), lambda b: (b, 0, 0))],
        out_shape=[jax.ShapeDtypeStruct((B, NSA_HEADS, NSA_HD), F32),
                   jax.ShapeDtypeStruct((B, NSA_KV, n_sel), jnp.int32)],
        compiler_params=_cparams("parallel"),
        name="nsa_sample_select",
    )(q, kc, vc)


N_PAST_SEL = N_SEL - 1
SEL_K_COL = 2 * CMP_W
SEL_V_COL = 3 * CMP_W
BLOCKS_PER_PAGE = PAGE_SIZE // SEL_BLOCK


def _sel_copies(pt_ref, idx_ref, cache_ref, kbuf, vbuf, sem, layer, n_pages, b, slot):
    out = []
    for g in range(NSA_KV):
        for t in range(N_PAST_SEL):
            blk = idx_ref[(b * NSA_KV + g) * N_SEL + t]
            pg = pt_ref[b * n_pages + blk // BLOCKS_PER_PAGE]
            rows = pl.ds((blk % BLOCKS_PER_PAGE) * SEL_BLOCK, SEL_BLOCK)
            dst = pl.ds(t * SEL_BLOCK, SEL_BLOCK)
            plane = (g // GROUPS_PER_PLANE) * LANES
            out.append(pltpu.make_async_copy(cache_ref.at[layer, pg, rows, pl.ds(SEL_K_COL + plane, LANES)],
                                             kbuf.at[slot, g, dst, :], sem.at[slot]))
            out.append(pltpu.make_async_copy(cache_ref.at[layer, pg, rows, pl.ds(SEL_V_COL + plane, LANES)],
                                             vbuf.at[slot, g, dst, :], sem.at[slot]))
    return out


def _softmax_with_self(s, s_self, mask=None):
    if mask is not None:
        s = jnp.where(mask, s, NEG)
    m = jnp.maximum(jnp.max(s, axis=-1, keepdims=True), s_self)
    e = jnp.exp(s - m)
    if mask is not None:
        e = e * mask.astype(F32)
    e_self = jnp.exp(s_self - m)
    return e, e_self, jnp.sum(e, axis=-1, keepdims=True) + e_self


def _nsa_sample_attend_kernel(layer, n_pages, win_first, pt_ref, idx_ref, cache_ref, qr_ref, new_ref, win_ref, oc_ref,
                              gate_ref, o_ref, kbuf, vbuf, sem):
    b = pl.program_id(0)
    nb = pl.num_programs(0)
    slot = b % 2
    mk = functools.partial(_sel_copies, pt_ref, idx_ref, cache_ref, kbuf, vbuf, sem, layer, n_pages)

    @pl.when(b == 0)
    def _():
        for cp in mk(b, slot):
            cp.start()

    @pl.when(b + 1 < nb)
    def _():
        for cp in mk(b + 1, 1 - slot):
            cp.start()

    for cp in mk(b, slot):
        cp.wait()

    qr = qr_ref[...]
    qf = qr.astype(F32)
    head_group = lax.broadcasted_iota(jnp.int32, (NSA_HEADS, 1), 0) // NSA_REP
    gates = _sigmoid(gate_ref[...])
    n_win = win_ref.shape[0]
    wmask = lax.broadcasted_iota(jnp.int32, (NSA_HEADS, n_win), 1) >= win_first
    out = jnp.zeros((NSA_HEADS, NSA_HD), F32)
    for g in range(NSA_KV):
        cols = slice(g * NSA_HD, (g + 1) * NSA_HD)
        ks_new, vs_new = new_ref[0:1, cols], new_ref[1:2, cols]
        kw_new, vw_new = new_ref[2:3, cols], new_ref[3:4, cols]
        half = slice((g % GROUPS_PER_PLANE) * NSA_HD, (g % GROUPS_PER_PLANE + 1) * NSA_HD)
        s = lax.dot_general(qr, kbuf[slot, g][:, half].astype(BF16), NT_DIMS, preferred_element_type=F32)
        e, e_self, d = _softmax_with_self(s, jnp.sum(qf * ks_new, axis=-1, keepdims=True))
        o_s = (jnp.dot(e.astype(BF16), vbuf[slot, g][:, half].astype(BF16), preferred_element_type=F32)
               + e_self * vs_new) / d
        s = lax.dot_general(qr, win_ref[:, cols].astype(BF16), NT_DIMS, preferred_element_type=F32)
        e, e_self, d = _softmax_with_self(s, jnp.sum(qf * kw_new, axis=-1, keepdims=True), wmask)
        vw = win_ref[:, CMP_W + g * NSA_HD:CMP_W + (g + 1) * NSA_HD].astype(BF16)
        o_w = (jnp.dot(e.astype(BF16), vw, preferred_element_type=F32) + e_self * vw_new) / d
        o = gates[:, 0:1] * oc_ref[...] + gates[:, 1:2] * o_s + gates[:, 2:3] * o_w
        out = jnp.where(head_group == g, o, out)
    o_ref[...] = out


def nsa_sample_attend(cache, layer, page_table, idx, qr, new_rows, win, o_cmp, gates, win_first):
    B, n_pages = page_table.shape
    n_win = win.shape[2]
    hspec = lambda: pl.BlockSpec((None, NSA_HEADS, NSA_HD), lambda b, pt, ix: (b, 0, 0))
    return pl.pallas_call(
        functools.partial(_nsa_sample_attend_kernel, layer, n_pages, win_first),
        grid_spec=pltpu.PrefetchScalarGridSpec(
            num_scalar_prefetch=2,
            grid=(B,),
            in_specs=[pl.BlockSpec(memory_space=pl.ANY),
                      hspec(),
                      pl.BlockSpec((None, 4, CMP_W), lambda b, pt, ix: (b, 0, 0)),
                      pl.BlockSpec((None, None, n_win, 2 * CMP_W), lambda b, pt, ix: (layer, b, 0, 0)),
                      hspec(),
                      pl.BlockSpec((None, NSA_HEADS, LANES), lambda b, pt, ix: (b, 0, 0))],
            out_specs=hspec(),
            scratch_shapes=[pltpu.VMEM((2, NSA_KV, N_PAST_SEL * SEL_BLOCK, LANES), F32),
                            pltpu.VMEM((2, NSA_KV, N_PAST_SEL * SEL_BLOCK, LANES), F32),
                            pltpu.SemaphoreType.DMA((2,))]),
        out_shape=jax.ShapeDtypeStruct((B, NSA_HEADS, NSA_HD), F32),
        compiler_params=_cparams("arbitrary"),
        name="nsa_sample_attend",
    )(page_table.reshape(-1), idx.reshape(-1), cache, qr, new_rows, win, o_cmp, gates)


def _merge_kernel(oa_ref, ob_ref, wa_ref, wb_ref, ma_ref, mb_ref, o_ref, wa_bf, wb_bf):
    @pl.when(pl.program_id(1) == 0)
    def _():
        wa_bf[...] = wa_ref[...].astype(BF16)
        wb_bf[...] = wb_ref[...].astype(BF16)

    ya = jnp.dot(oa_ref[...], wa_bf[...], preferred_element_type=F32)
    yb = jnp.dot(ob_ref[...], wb_bf[...], preferred_element_type=F32)
    o_ref[...] = (_sigmoid(ma_ref[...]) * ya + _sigmoid(mb_ref[...]) * yb).astype(o_ref.dtype)


def merge(o_a, o_b, w_a, w_b, zm, layer, *, tm, tn=512):
    M, K = o_a.shape
    D = w_a.shape[-1]
    xspec = pl.BlockSpec((tm, K), lambda j, i: (i, 0))
    wspec = pl.BlockSpec((None, K, tn), lambda j, i: (layer, 0, j))
    return pl.pallas_call(
        _merge_kernel,
        grid=(D // tn, M // tm),
        in_specs=[xspec, xspec, wspec, wspec,
                  pl.BlockSpec((tm, tn), lambda j, i: (i, j)),
                  pl.BlockSpec((tm, tn), lambda j, i: (i, j + D // tn))],
        out_specs=pl.BlockSpec((tm, tn), lambda j, i: (i, j)),
        out_shape=jax.ShapeDtypeStruct((M, D), BF16),
        scratch_shapes=[pltpu.VMEM((K, tn), BF16), pltpu.VMEM((K, tn), BF16)],
        compiler_params=_cparams("arbitrary", "arbitrary"),
        name="merge",
    )(o_a, o_b, w_a, w_b, zm, zm)


MOE_TILE = 256
ROUTE_TILE = 256


def _first_argmax(vals):
    best, idx = vals[0], jnp.zeros(vals[0].shape, F32)
    for e in range(1, len(vals)):
        better = vals[e] > best
        idx = jnp.where(better, float(e), idx)
        best = jnp.where(better, vals[e], best)
    return best, idx


def _router_kernel(x_ref, w_ref, b_ref, o_ref):
    xs = _split3(x_ref[...])
    ws = _split3(w_ref[...])
    terms = ((0, 0), (0, 1), (1, 0), (0, 2), (2, 0), (1, 1))
    logits = b_ref[...] + sum(lax.dot_general(ws[a], xs[c], NT_DIMS, preferred_element_type=F32) for a, c in terms)
    m = jnp.max(logits, axis=0, keepdims=True)
    e = jnp.exp(logits - m)
    probs = e / jnp.sum(e, axis=0, keepdims=True)
    p = [probs[i:i + 1] for i in range(N_EXPERTS)]
    gscore = []
    for g in range(N_GROUPS):
        mem = p[g * EXPERTS_PER_GROUP:(g + 1) * EXPERTS_PER_GROUP]
        pairs = [mem[a] + mem[c] for a in range(EXPERTS_PER_GROUP) for c in range(a + 1, EXPERTS_PER_GROUP)]
        gscore.append(functools.reduce(jnp.maximum, pairs))
    _, g_sel = _first_argmax(gscore)
    masked = [jnp.where(g_sel == float(i // EXPERTS_PER_GROUP), p[i], -1.0) for i in range(N_EXPERTS)]
    v1, i1 = _first_argmax(masked)
    masked2 = [jnp.where(i1 == float(i), -2.0, masked[i]) for i in range(N_EXPERTS)]
    v2, i2 = _first_argmax(masked2)
    tot = v1 + v2
    o_ref[...] = jnp.concatenate([i1, i2, v1 / tot, v2 / tot, jnp.zeros((4,) + i1.shape[1:], F32)], axis=0)


def moe_route(x, w_router, b_router):
    T, D = x.shape
    return pl.pallas_call(
        _router_kernel,
        grid=(T // ROUTE_TILE,),
        in_specs=[pl.BlockSpec((ROUTE_TILE, D), lambda i: (i, 0)),
                  pl.BlockSpec((N_EXPERTS, D), lambda i: (0, 0)),
                  pl.BlockSpec((N_EXPERTS, 1), lambda i: (0, 0))],
        out_specs=pl.BlockSpec((8, ROUTE_TILE), lambda i: (0, i)),
        out_shape=jax.ShapeDtypeStruct((8, T), F32),
        compiler_params=_cparams("parallel"),
        name="moe_route",
    )(x, w_router.T, b_router.reshape(N_EXPERTS, 1))


SUBLANES = 8


def _expert_row_copies(start, hbm, rows_ref, base, buf, sub, sem, to_hbm):
    def body(j, carry):
        idx = rows_ref[base + j]

        @pl.when(idx >= 0)
        def _():
            win = buf.at[pl.ds(pl.multiple_of(j * sub, sub), sub), :]
            row = hbm.at[idx]
            cp = pltpu.make_async_copy(win, row, sem) if to_hbm else pltpu.make_async_copy(row, win, sem)
            if start:
                cp.start()
            else:
                cp.wait()

        return carry

    lax.fori_loop(0, MOE_TILE, body, 0, unroll=8)


def _expert_ffn_kernel(src_ref, dst_ref, te_ref, nt_ref, xw_hbm, w1_ref, w3_ref, w2_ref, y_hbm,
                       xbuf, x2d, y2d, obuf, gsem, ssem):
    t = pl.program_id(0)
    nt = nt_ref[0]
    slot = t % 2
    xsub = xw_hbm.shape[1]
    ysub = y_hbm.shape[1]
    gather = lambda start, tile, sl: _expert_row_copies(start, xw_hbm, src_ref, tile * MOE_TILE, xbuf.at[sl], xsub,
                                                        gsem.at[sl], False)
    scatter = lambda start, tile: _expert_row_copies(start, y_hbm, dst_ref, tile * MOE_TILE, obuf, ysub, ssem, True)

    @pl.when((t == 0) & (nt > 0))
    def _():
        gather(True, t, slot)

    @pl.when(t + 1 < nt)
    def _():
        gather(True, t + 1, 1 - slot)

    @pl.when(t < nt)
    def _():
        gather(False, t, slot)
        xs = xbuf.at[slot]
        for r in range(MOE_TILE // SUBLANES):
            for c in range(xsub):
                x2d[r * SUBLANES:(r + 1) * SUBLANES, c * LANES:(c + 1) * LANES] = (
                    xs[pl.ds(r * SUBLANES * xsub + c, SUBLANES, stride=xsub), :])
        x = x2d[...].astype(BF16)
        h = _silu(jnp.dot(x, w1_ref[...], preferred_element_type=F32)) * jnp.dot(x, w3_ref[...],
                                                                              preferred_element_type=F32)
        y2d[...] = jnp.dot(h.astype(BF16), w2_ref[...], preferred_element_type=F32)

        @pl.when(t > 0)
        def _():
            scatter(False, t - 1)

        for r in range(MOE_TILE // SUBLANES):
            for c in range(ysub):
                obuf[pl.ds(r * SUBLANES * ysub + c, SUBLANES, stride=ysub), :] = (
                    y2d[r * SUBLANES:(r + 1) * SUBLANES, c * LANES:(c + 1) * LANES])
        scatter(True, t)

        @pl.when(t == nt - 1)
        def _():
            scatter(False, t)


def expert_ffn(xw, src_token, dst_row, n_rows_out, tile_expert, n_tiles_used, w1, w3, w2, layer):
    P = src_token.shape[0]
    xsub = xw.shape[1]
    D, F = w1.shape[2], w1.shape[3]
    ysub = D // LANES
    wspec = lambda a, c: pl.BlockSpec((None, None, a, c), lambda t, src, dst, te, nt: (layer, te[t], 0, 0))
    return pl.pallas_call(
        _expert_ffn_kernel,
        grid_spec=pltpu.PrefetchScalarGridSpec(
            num_scalar_prefetch=4,
            grid=(P // MOE_TILE,),
            in_specs=[pl.BlockSpec(memory_space=pl.ANY), wspec(D, F), wspec(D, F), wspec(F, D)],
            out_specs=pl.BlockSpec(memory_space=pl.ANY),
            scratch_shapes=[pltpu.VMEM((2, MOE_TILE * xsub, LANES), F32),
                            pltpu.VMEM((MOE_TILE, xsub * LANES), F32),
                            pltpu.VMEM((MOE_TILE, D), F32),
                            pltpu.VMEM((MOE_TILE * ysub, LANES), F32),
                            pltpu.SemaphoreType.DMA((2,)),
                            pltpu.SemaphoreType.DMA(())]),
        out_shape=jax.ShapeDtypeStruct((n_rows_out, ysub, LANES), F32),
        compiler_params=_cparams("arbitrary"),
        name="expert_ffn",
    )(src_token, dst_row, tile_expert, n_tiles_used, xw, w1, w3, w2)


def _combine_ln_kernel(alpha, x_ref, y1_ref, y2_ref, wt_ref, g_ref, b_ref, o_ref, ob_ref):
    wt = wt_ref[...]
    y = wt[:, 0:1] * y1_ref[...] + wt[:, 1:2] * y2_ref[...]
    v = alpha * x_ref[...] + y
    mu = jnp.mean(v, axis=-1, keepdims=True)
    d = v - mu
    var = jnp.mean(d * d, axis=-1, keepdims=True)
    out = d * lax.rsqrt(var + LN_EPS) * g_ref[...] + b_ref[...]
    o_ref[...] = out
    ob_ref[...] = out.astype(BF16)


def moe_combine_layer_norm(x, ysel, wts, g, b, alpha, *, tm=256):
    T, D = x.shape
    nt = T // tm
    spec = pl.BlockSpec((tm, D), lambda i: (i, 0))
    vec = pl.BlockSpec((1, D), lambda i: (0, 0))
    return pl.pallas_call(
        functools.partial(_combine_ln_kernel, alpha),
        grid=(nt,),
        in_specs=[spec, spec, pl.BlockSpec((tm, D), lambda i: (i + nt, 0)),
                  pl.BlockSpec((tm, 2), lambda i: (i, 0)), vec, vec],
        out_specs=[spec, spec],
        out_shape=[jax.ShapeDtypeStruct((T, D), F32), jax.ShapeDtypeStruct((T, D), BF16)],
        compiler_params=_cparams("parallel"),
        name="moe_combine_ln",
    )(x, ysel, ysel, wts, g.reshape(1, D), b.reshape(1, D))


def moe_dispatch_plan(e1, e2, n_tiles):
    T = e1.shape[0]
    ea = jnp.concatenate([e1, e2])
    onehot = (ea[:, None] == jnp.arange(N_EXPERTS)[None, :]).astype(jnp.int32)
    rank = jnp.take_along_axis(jnp.cumsum(onehot, axis=0) - onehot, ea[:, None], axis=1)[:, 0]
    counts = onehot.sum(0)
    tiles = (counts + MOE_TILE - 1) // MOE_TILE
    tile_end = jnp.cumsum(tiles)
    dest = ((tile_end - tiles) * MOE_TILE)[ea] + rank
    n_slots = n_tiles * MOE_TILE
    src_token = jnp.zeros((n_slots,), jnp.int32).at[dest].set(jnp.tile(jnp.arange(T, dtype=jnp.int32), 2))
    dst_row = jnp.full((n_slots,), -1, jnp.int32).at[dest].set(jnp.arange(2 * T, dtype=jnp.int32))
    tile_expert = jnp.minimum((tile_end[None, :] <= jnp.arange(n_tiles)[:, None]).sum(-1), N_EXPERTS - 1)
    return src_token, dst_row, tile_expert.astype(jnp.int32), tile_end[-1:].astype(jnp.int32)


def moe_block(x, xb, w_router, b_router, w1, w3, w2, layer, g, b, alpha):
    T, D = x.shape
    r = moe_route(x, w_router, b_router)
    e1, e2 = r[0].astype(jnp.int32), r[1].astype(jnp.int32)
    n_tiles = 2 * T // MOE_TILE + N_EXPERTS
    src_token, dst_row, tile_expert, n_used = moe_dispatch_plan(e1, e2, n_tiles)
    y = expert_ffn(x.reshape(T, D // LANES, LANES), src_token, dst_row, 2 * T, tile_expert, n_used, w1, w3, w2, layer)
    ysel = y.reshape(2 * T, D)
    return moe_combine_layer_norm(x, ysel, r[2:4].T, g, b, alpha)


def _hgrn_sample_kernel(qt_ref, ft_ref, lbt_ref, onemt_ref, ia_ref, ga_ref, gn_ref, s0_ref, o_ref, s_ref):
    qa = qt_ref[...]
    fa = ft_ref[...]
    q_all = _silu(qa)
    sig = _sigmoid(fa)
    f_all = lbt_ref[...] + onemt_ref[...] * sig
    k_all = onemt_ref[...] * _sigmoid(-fa)
    for h in range(A_HEADS):
        cols = slice(h * A_DK, (h + 1) * A_DK)
        v = ia_ref[:, cols]
        s_new = f_all[:, h:h + 1] * s0_ref[h] + k_all[:, h:h + 1] * v
        s_ref[h] = s_new
        o = jnp.sum(s_new * q_all[:, h:h + 1], axis=0, keepdims=True)
        o_ref[:, cols] = _hgrn_gate_norm(o, ga_ref[:, cols], gn_ref[:, cols]).astype(o_ref.dtype)


def hgrn_sample(qa_t, fa_t, lb_t, onem_t, ia, ga, gnorm, state, layer):
    B = qa_t.shape[0]
    W = A_HEADS * A_DK
    col = pl.BlockSpec((None, A_DK, A_HEADS), lambda b: (b, 0, 0))
    par = pl.BlockSpec((A_DK, A_HEADS), lambda b: (0, 0))
    row = pl.BlockSpec((None, 1, W), lambda b: (b, 0, 0))
    return pl.pallas_call(
        _hgrn_sample_kernel,
        grid=(B,),
        in_specs=[col, col, par, par, row, row, pl.BlockSpec((1, W), lambda b: (0, 0)),
                  pl.BlockSpec((None, None, A_HEADS, A_DK, A_DK), lambda b: (layer, b, 0, 0, 0))],
        out_specs=[row, pl.BlockSpec((None, A_HEADS, A_DK, A_DK), lambda b: (b, 0, 0, 0))],
        out_shape=[jax.ShapeDtypeStruct((B, 1, W), BF16), jax.ShapeDtypeStruct((B, A_HEADS, A_DK, A_DK), F32)],
        compiler_params=_cparams("parallel"),
        name="hgrn_sample",
    )(qa_t, fa_t, lb_t, onem_t, ia, ga, gnorm, state)


MM_TILE_M = 768
MM_TILE_N = 512
W_A = A_HEADS * A_DK
W_Q = NSA_HEADS * NSA_HD
COL_QB = 4 * W_A
COL_KV = COL_QB + W_Q
COL_GB = COL_KV + 6 * CMP_W
COL_MA = COL_GB + 3 * NSA_HEADS


def _pad_lanes(a):
    return jnp.pad(a, [(0, 0)] * (a.ndim - 1) + [(0, LANES - a.shape[-1])])

def kernel(x_prompt, x_sample, cache_nsa_kv, cache_nsa_win, state_hgrn, page_table, w_in, lb_raw, gnorm_a, cmp_pe, cmp_w1, cmp_w2, w_branch_a, w_branch_b, w_out, ln1_g, ln1_b, ln2_g, ln2_b, w_router, b_router, w_e1, w_e3, w_e2):
    depth, D = w_in.shape[0], w_in.shape[1]
    L, B = x_prompt.shape[1], x_sample.shape[0]
    assert x_prompt.shape[0] == 1 and x_sample.shape[1] == 1
    n_pool = cache_nsa_kv.shape[1]
    past_len = page_table.shape[1] * PAGE_SIZE
    win_rows = cache_nsa_win.shape[2]
    T = L + B
    M = -(-T // MM_TILE_M) * MM_TILE_M
    alpha = (2 * depth) ** 0.25
    prompt, sample = slice(0, L), slice(L, T)

    lbs = jnp.cumsum(jax.nn.softmax(lb_raw.astype(F32), axis=0), axis=0)
    lbs = lbs - lbs[0:1]
    pos = jnp.concatenate([jnp.arange(L), jnp.full((B,), past_len), jnp.zeros((M - T,), jnp.int32)])
    cos, sin = rope_tables(pos)
    ones, zeros = jnp.ones_like(cos), jnp.zeros_like(cos)
    cache = cache_nsa_kv.reshape(depth, n_pool, PAGE_SIZE, 4 * CMP_W)
    win_cache = cache_nsa_win.reshape(depth, B, win_rows, 2 * CMP_W)
    w1b, w3b, w2b = w_e1.astype(BF16), w_e3.astype(BF16), w_e2.astype(BF16)
    scale = NSA_HD ** -0.5

    x = jnp.concatenate([x_prompt[0], x_sample[:, 0], jnp.zeros((M - T, D), F32)])
    xb = x.astype(BF16)
    heads_major = lambda a, h: a.reshape(a.shape[0], h, NSA_HD).transpose(1, 0, 2)
    col_major = lambda a: a.reshape(B, A_HEADS, A_DK).transpose(0, 2, 1)
    outs = [[] for _ in range(6)]
    for l in range(depth):
        lb = lbs[l]
        mm = functools.partial(matmul, tm=MM_TILE_M, tn=MM_TILE_N)
        z = mm(xb, w_in, l, col0=0, ncols=COL_GB)
        zg = matmul(xb, _pad_lanes(w_in[l, :, COL_GB:COL_MA])[None], 0, tm=MM_TILE_M, tn=LANES)
        zm = mm(xb, w_in[l, :, COL_MA:][None], 0)

        o_a_p, st_p = hgrn_prompt(z, L, jnp.log(lb)[None], jnp.log1p(-lb)[None], (1.0 - lb)[None], gnorm_a[l][None])
        zs = z[sample]
        o_a_s, st_s = hgrn_sample(col_major(zs[:, :W_A]), col_major(zs[:, W_A:2 * W_A]),
                                  lb.reshape(A_HEADS, A_DK).T, (1.0 - lb).reshape(A_HEADS, A_DK).T,
                                  zs[:, None, 2 * W_A:3 * W_A], zs[:, None, 3 * W_A:4 * W_A], gnorm_a[l][None],
                                  state_hgrn, l)
        o_a = jnp.concatenate([o_a_p, o_a_s[:, 0], jnp.zeros((M - T, W_A), BF16)])

        q_plain = rope(z, COL_QB, W_Q, ones, zeros, scale=scale, out_dtype=BF16)
        q_rot = rope(z, COL_QB, W_Q, cos, sin, scale=scale, out_dtype=BF16)
        ks_rot = rope(z, COL_KV + 2 * CMP_W, CMP_W, cos, sin)
        kw_rot = rope(z, COL_KV + 4 * CMP_W, CMP_W, cos, sin)
        v_sel = z[:, COL_KV + 3 * CMP_W:COL_KV + 4 * CMP_W]
        v_win = z[:, COL_KV + 5 * CMP_W:COL_KV + 6 * CMP_W]
        rows = jnp.concatenate([z[:, COL_KV:COL_KV + 2 * CMP_W], ks_rot, v_sel], axis=-1)
        win_new = jnp.concatenate([kw_rot, v_win], axis=-1)
        cw = compress_weights(cmp_pe[l], cmp_w1[l], cmp_w2[l])

        kc_p, vc_p = compress_paged(rows[prompt].reshape(1, L // PAGE_SIZE, PAGE_SIZE, 4 * CMP_W), 0,
                                    jnp.arange(L // PAGE_SIZE, dtype=jnp.int32)[None], *cw)
        grp = lambda a: heads_major(a, NSA_KV).astype(BF16)
        gates_p =_pad_lanes(zg[prompt, :3 * NSA_HEADS].reshape(L, NSA_KV, 3 * NSA_REP).transpose(1, 0, 2))
        o_b_p = nsa_prompt(heads_major(q_plain[prompt], NSA_HEADS), heads_major(q_rot[prompt], NSA_HEADS),
                           grp(kc_p[0]), grp(vc_p[0]), grp(ks_rot[prompt]), grp(v_sel[prompt]),
                           grp(kw_rot[prompt]), grp(v_win[prompt]), gates_p)

        kc_s, vc_s = compress_paged(cache, l, page_table, *cw)
        o_c_s, idx = nsa_sample_select(q_plain[sample].reshape(B, NSA_HEADS, NSA_HD), kc_s, vc_s,
                                       past_len, past_len // SEL_BLOCK)
        rs, ws = rows[sample], win_new[sample]
        new_rows = jnp.stack([rs[:, 2 * CMP_W:3 * CMP_W], rs[:, 3 * CMP_W:], ws[:, :CMP_W], ws[:, CMP_W:]], axis=1)
        gates_s = _pad_lanes(zg[sample, :3 * NSA_HEADS].reshape(B, NSA_HEADS, 3))
        o_b_s = nsa_sample_attend(cache, l, page_table, idx[:, :, :N_SEL], q_rot[sample].reshape(B, NSA_HEADS, NSA_HD),
                                  new_rows, win_cache, o_c_s, gates_s, win_rows - WINDOW + 1)
        o_b = jnp.concatenate([o_b_p, o_b_s.reshape(B, W_Q).astype(BF16), jnp.zeros((M - T, W_Q), BF16)])

        h = merge(o_a, o_b, w_branch_a, w_branch_b, zm, l, tm=MM_TILE_M)
        y = mm(h, w_out, l)
        x1, x1b = residual_layer_norm(x, y, ln1_g[l], ln1_b[l], alpha)
        x, xb = moe_block(x1, x1b, w_router, b_router, w1b, w3b, w2b, l, ln2_g[l], ln2_b[l], alpha)

        wbp = min(WINDOW, L)
        outs[0].append(rows[prompt].reshape(1, L, 4, NSA_KV, NSA_HD))
        outs[1].append(rs.reshape(B, 1, 4, NSA_KV, NSA_HD))
        outs[2].append(win_new[L - wbp:L].reshape(1, wbp, 2, NSA_KV, NSA_HD))
        outs[3].append(jnp.concatenate([cache_nsa_win[l][:, 1:], ws.reshape(B, 1, 2, NSA_KV, NSA_HD)], axis=1))
        outs[4].append(st_p[None])
        outs[5].append(st_s)
    return (x[prompt][None], x[sample][:, None]) + tuple(jnp.stack(o) for o in outs)
```

```python
import functools
import math

import jax
import jax.numpy as jnp
import numpy as np
from jax import lax
from jax.experimental import pallas as pl
from jax.experimental.pallas import tpu as pltpu

F32 = jnp.float32
BF16 = jnp.bfloat16

A_HEADS = 8
A_DK = 128
NSA_HEADS = 16
NSA_KV = 4
NSA_REP = NSA_HEADS // NSA_KV
NSA_HD = 64
CMP_BLOCK = 32
CMP_STRIDE = 16
CMP_HIDDEN = 128
SEL_BLOCK = 64
N_SEL = 16
WINDOW = 512
PAGE_SIZE = 128
ROPE_THETA = 10000.0
N_EXPERTS = 16
N_GROUPS = 4
EXPERTS_PER_GROUP = N_EXPERTS // N_GROUPS
LN_EPS = 1e-5
NEG = -1e30

VMEM_LIMIT_BYTES = 56 * 1024 * 1024
LANES = 128


def _cparams(*sem):
    return pltpu.CompilerParams(dimension_semantics=sem, vmem_limit_bytes=VMEM_LIMIT_BYTES)


def _sigmoid(x):
    return 1.0 / (1.0 + jnp.exp(-x))


def _silu(x):
    return x * _sigmoid(x)


def _mm_kernel(x_ref, w_ref, o_ref, wb_ref):
    @pl.when(pl.program_id(1) == 0)
    def _():
        wb_ref[...] = w_ref[...].astype(BF16)

    o_ref[...] = jnp.dot(x_ref[...], wb_ref[...], preferred_element_type=F32).astype(o_ref.dtype)


def matmul(x, w, layer, *, tm, tn, col0=0, ncols=None, out_dtype=F32):
    M, K = x.shape
    ncols = w.shape[-1] if ncols is None else ncols
    assert M % tm == 0 and ncols % tn == 0 and col0 % tn == 0
    off = col0 // tn
    return pl.pallas_call(
        _mm_kernel,
        grid=(ncols // tn, M // tm),
        in_specs=[pl.BlockSpec((tm, K), lambda j, i: (i, 0)),
                  pl.BlockSpec((None, K, tn), lambda j, i: (layer, 0, j + off))],
        out_specs=pl.BlockSpec((tm, tn), lambda j, i: (i, j)),
        out_shape=jax.ShapeDtypeStruct((M, ncols), out_dtype),
        scratch_shapes=[pltpu.VMEM((K, tn), BF16)],
        compiler_params=_cparams("arbitrary", "arbitrary"),
        name="matmul",
    )(x, w)


SUBLANES = 8


def _layer_norm(v, g, b):
    mu = jnp.mean(v, axis=-1, keepdims=True)
    d = v - mu
    var = jnp.mean(d * d, axis=-1, keepdims=True)
    return d * lax.rsqrt(var + LN_EPS) * g + b


def _ln_kernel(alpha, x_ref, y_ref, g_ref, b_ref, o_ref, slab_ref):
    o_ref[...] = _layer_norm(alpha * x_ref[...] + y_ref[...], g_ref[...], b_ref[...])
    tm, D = o_ref.shape
    sub = D // LANES
    for r in range(tm // SUBLANES):
        for c in range(sub):
            slab_ref[pl.ds(r * SUBLANES * sub + c, SUBLANES, stride=sub), :] = (
                o_ref[r * SUBLANES:(r + 1) * SUBLANES, c * LANES:(c + 1) * LANES])


def residual_layer_norm(x, y, g, b, alpha, *, tm=256):
    M, D = x.shape
    spec = pl.BlockSpec((tm, D), lambda i: (i, 0))
    vec = pl.BlockSpec((1, D), lambda i: (0, 0))
    return pl.pallas_call(
        functools.partial(_ln_kernel, alpha),
        grid=(M // tm,),
        in_specs=[spec, spec, vec, vec],
        out_specs=[spec, pl.BlockSpec((tm * D // LANES, LANES), lambda i: (i, 0))],
        out_shape=[jax.ShapeDtypeStruct((M, D), F32), jax.ShapeDtypeStruct((M * D // LANES, LANES), F32)],
        compiler_params=_cparams("parallel"),
        name="residual_ln",
    )(x, y, g.reshape(1, D), b.reshape(1, D))


def _rope_kernel(scale, x_ref, cos_ref, sin_ref, o_ref):
    cos = cos_ref[...]
    sin = sin_ref[...]
    lane = lax.broadcasted_iota(jnp.int32, cos.shape, 1)
    first_half = (lane % NSA_HD) < (NSA_HD // 2)
    for j in range(x_ref.shape[1] // LANES):
        x = x_ref[:, j * LANES:(j + 1) * LANES]
        partner = jnp.where(first_half, pltpu.roll(x, LANES - NSA_HD // 2, 1), pltpu.roll(x, NSA_HD // 2, 1))
        o_ref[:, j * LANES:(j + 1) * LANES] = ((x * cos + partner * sin) * scale).astype(o_ref.dtype)


def rope(z, col0, width, cos, sin, *, scale=1.0, out_dtype=F32, tm=256):
    M = z.shape[0]
    assert col0 % width == 0 and width % LANES == 0
    return pl.pallas_call(
        functools.partial(_rope_kernel, scale),
        grid=(M // tm,),
        in_specs=[pl.BlockSpec((tm, width), lambda i: (i, col0 // width)),
                  pl.BlockSpec((tm, LANES), lambda i: (i, 0)),
                  pl.BlockSpec((tm, LANES), lambda i: (i, 0))],
        out_specs=pl.BlockSpec((tm, width), lambda i: (i, 0)),
        out_shape=jax.ShapeDtypeStruct((M, width), out_dtype),
        compiler_params=_cparams("parallel"),
        name="rope",
    )(z, cos, sin)


def rope_tables(pos):
    half = NSA_HD // 2
    inv = jnp.power(ROPE_THETA, -jnp.arange(half, dtype=F32) / half)
    ang = pos.astype(F32)[:, None] * inv[None, :]
    cos, sin = jnp.cos(ang), jnp.sin(ang)
    reps = LANES // NSA_HD
    return jnp.tile(jnp.concatenate([cos, cos], -1), (1, reps)), jnp.tile(jnp.concatenate([-sin, sin], -1), (1, reps))


HG_CHUNK = 64
HG_SUB = 16
HG_BLOCK = 256


def _split3(x):
    hi = x.astype(BF16)
    r1 = x - hi.astype(F32)
    mid = r1.astype(BF16)
    lo = (r1 - mid.astype(F32)).astype(BF16)
    return hi, mid, lo


def _log_forget(fa, loglb, log1mlb):
    log_sig = jnp.minimum(fa, 0.0) - jnp.log(1.0 + jnp.exp(-jnp.abs(fa)))
    b = log1mlb + log_sig
    mx = jnp.maximum(loglb, b)
    return mx + jnp.log(1.0 + jnp.exp(-jnp.abs(loglb - b)))


def _hgrn_gate_norm(o, ga, gn):
    o = o * lax.rsqrt(jnp.mean(o * o, axis=-1, keepdims=True) + 1e-6)
    return o * gn * _silu(ga)


def _hgrn_prompt_kernel(qa_ref, fa_ref, ia_ref, ga_ref, loglb_ref, log1mlb_ref, onemlb_ref, gn_ref,
                        o_ref, s_ref, st_ref):
    c = pl.program_id(1)

    @pl.when(c == 0)
    def _():
        st_ref[...] = jnp.zeros_like(st_ref)

    loglb, log1mlb, onemlb, gn = loglb_ref[...], log1mlb_ref[...], onemlb_ref[...], gn_ref[...]
    row = lax.broadcasted_iota(jnp.int32, (HG_CHUNK, HG_CHUNK), 0)
    col = lax.broadcasted_iota(jnp.int32, (HG_CHUNK, HG_CHUNK), 1)
    tril = (col <= row).astype(BF16)
    srow = lax.broadcasted_iota(jnp.int32, (HG_SUB, HG_SUB), 0)
    scol = lax.broadcasted_iota(jnp.int32, (HG_SUB, HG_SUB), 1)

    def chunk(ci, carry):
        r0 = pl.multiple_of(ci * HG_CHUNK, HG_CHUNK)
        rows = pl.ds(r0, HG_CHUNK)
        qa, fa, v = qa_ref[rows, :], fa_ref[rows, :], ia_ref[rows, :]
        q = _silu(qa)
        logf = _log_forget(fa, loglb, log1mlb)
        k = onemlb * _sigmoid(-fa)
        b = sum(jnp.dot(tril, t, preferred_element_type=F32) for t in _split3(logf))
        st = st_ref[...]
        vb = v.astype(BF16)
        o_inter = lax.dot_general((q * jnp.exp(b)).astype(BF16), st.astype(BF16), (((1,), (1,)), ((), ())),
                                  preferred_element_type=F32)
        for i in range(HG_CHUNK // HG_SUB):
            lo = i * HG_SUB
            qi, bi, ki = q[lo:lo + HG_SUB], b[lo:lo + HG_SUB], k[lo:lo + HG_SUB]
            o = o_inter[lo:lo + HG_SUB]
            if i > 0:
                anchor = b[lo - 1:lo]
                qt = (qi * jnp.exp(bi - anchor)).astype(BF16)
                kt = (k[:lo] * jnp.exp(anchor - b[:lo])).astype(BF16)
                a_off = lax.dot_general(qt, kt, (((1,), (1,)), ((), ())), preferred_element_type=F32)
                o = o + jnp.dot(a_off.astype(BF16), vb[:lo], preferred_element_type=F32)
            a_d = jnp.zeros((HG_SUB, HG_SUB), F32)
            for s in range(HG_SUB):
                p = qi * ki[s:s + 1] * jnp.exp(jnp.minimum(bi - bi[s:s + 1], 0.0))
                a_d = jnp.where(scol == s, jnp.sum(p, axis=-1, keepdims=True), a_d)
            a_d = jnp.where(scol <= srow, a_d, 0.0)
            o = o + jnp.dot(a_d.astype(BF16), vb[lo:lo + HG_SUB], preferred_element_type=F32)
            ga = ga_ref[pl.ds(r0 + lo, HG_SUB), :]
            o_ref[pl.ds(r0 + lo, HG_SUB), :] = _hgrn_gate_norm(o, ga, gn).astype(o_ref.dtype)
        b_last = b[HG_CHUNK - 1:HG_CHUNK]
        kt = (k * jnp.exp(b_last - b)).astype(BF16)
        st_ref[...] = st * jnp.exp(b_last) + lax.dot_general(vb, kt, (((0,), (0,)), ((), ())),
                                                             preferred_element_type=F32)
        return carry

    lax.fori_loop(0, HG_BLOCK // HG_CHUNK, chunk, 0, unroll=True)

    @pl.when(c == pl.num_programs(1) - 1)
    def _():
        s_ref[...] = st_ref[...].T


def hgrn_prompt(z, L, loglb, log1mlb, onemlb, gnorm):
    blk = lambda part: pl.BlockSpec((HG_BLOCK, A_DK), lambda h, c: (c, part * A_HEADS + h))
    vec = pl.BlockSpec((1, A_DK), lambda h, c: (0, h))
    return pl.pallas_call(
        _hgrn_prompt_kernel,
        grid=(A_HEADS, L // HG_BLOCK),
        in_specs=[blk(0), blk(1), blk(2), blk(3), vec, vec, vec, vec],
        out_specs=[pl.BlockSpec((HG_BLOCK, A_DK), lambda h, c: (c, h)),
                   pl.BlockSpec((None, A_DK, A_DK), lambda h, c: (h, 0, 0))],
        out_shape=[jax.ShapeDtypeStruct((L, A_HEADS * A_DK), BF16),
                   jax.ShapeDtypeStruct((A_HEADS, A_DK, A_DK), F32)],
        scratch_shapes=[pltpu.VMEM((A_DK, A_DK), F32)],
        compiler_params=_cparams("arbitrary", "arbitrary"),
        name="hgrn_prompt",
    )(z, z, z, z, loglb, log1mlb, onemlb, gnorm)


CMP_PAGES = 32
CMP_ROWS = CMP_PAGES * PAGE_SIZE
CMP_OUT = CMP_ROWS // CMP_STRIDE
CMP_W = NSA_KV * NSA_HD
CMP_PLANES = 2 * CMP_W // LANES
GROUPS_PER_PLANE = LANES // NSA_HD


def _compress_copies(pt_ref, cache_ref, xbuf, sem, layer, bb, cc, slot, n_pages):
    out = []
    nxt = jnp.minimum((cc + 1) * CMP_PAGES, n_pages - 1)
    for q in range(CMP_PLANES):
        cols = pl.ds(q * LANES, LANES)
        for j in range(CMP_PAGES):
            pg = pt_ref[bb, cc * CMP_PAGES + j]
            out.append(pltpu.make_async_copy(cache_ref.at[layer, pg, :, cols],
                                             xbuf.at[slot, q, pl.ds(j * PAGE_SIZE, PAGE_SIZE), :], sem.at[slot]))
        out.append(pltpu.make_async_copy(cache_ref.at[layer, pt_ref[bb, nxt], pl.ds(0, CMP_STRIDE), cols],
                                         xbuf.at[slot, q, pl.ds(CMP_ROWS, CMP_STRIDE), :], sem.at[slot]))
    return out


def _compress_kernel(layer, pt_ref, cache_ref, w1_hbm, pe_ref, w2_ref, kc_ref, vc_ref, xbuf, w1buf, xcat, sem, wsem):
    b, c = pl.program_id(0), pl.program_id(1)
    nb, nc = pl.num_programs(0), pl.num_programs(1)
    step = b * nc + c
    slot = step % 2
    n_pages = nc * CMP_PAGES
    mk = functools.partial(_compress_copies, pt_ref, cache_ref, xbuf, sem, layer)

    @pl.when(step == 0)
    def _():
        wcopy = pltpu.make_async_copy(w1_hbm, w1buf, wsem)
        wcopy.start()
        for cp in mk(b, c, slot, n_pages):
            cp.start()
        wcopy.wait()

    @pl.when(step + 1 < nb * nc)
    def _():
        last_c = c == nc - 1
        for cp in mk(jnp.where(last_c, b + 1, b), jnp.where(last_c, 0, c + 1), 1 - slot, n_pages):
            cp.start()

    for cp in mk(b, c, slot, n_pages):
        cp.wait()

    planes_per_out = CMP_PLANES // 2
    for t, o_ref in enumerate((kc_ref, vc_ref)):
        for q in range(planes_per_out):
            xs = xbuf.at[slot, t * planes_per_out + q]
            for p in range(CMP_BLOCK):
                x = xs[pl.ds(p, CMP_OUT, stride=CMP_STRIDE), :] + pe_ref[t, p]
                xcat[:, p * LANES:(p + 1) * LANES] = x.astype(BF16)
            h = jnp.dot(xcat[...], w1buf[t], preferred_element_type=F32)
            o_ref[:, q * LANES:(q + 1) * LANES] = jnp.dot(_silu(h).astype(BF16), w2_ref[t],
                                                          preferred_element_type=F32)


def compress_weights(pe, w1, w2):
    eye = jnp.eye(GROUPS_PER_PLANE, dtype=F32)
    w1r = w1.reshape(2, CMP_BLOCK, NSA_HD, CMP_HIDDEN)
    w1bd = jnp.einsum('gh,tpdn->tpgdhn', eye, w1r).reshape(2, CMP_BLOCK * LANES, GROUPS_PER_PLANE * CMP_HIDDEN)
    w2bd = jnp.einsum('gh,tnd->tgnhd', eye, w2).reshape(2, GROUPS_PER_PLANE * CMP_HIDDEN, LANES)
    pet = jnp.tile(pe, (1, 1, GROUPS_PER_PLANE)).reshape(2, CMP_BLOCK, 1, LANES)
    return w1bd.astype(BF16), pet, w2bd.astype(BF16)


def compress_paged(cache, layer, page_table, w1bd, pet, w2bd):
    B, n_pages = page_table.shape
    assert n_pages % CMP_PAGES == 0
    nc = n_pages // CMP_PAGES
    out = jax.ShapeDtypeStruct((B, nc * CMP_OUT, CMP_W), F32)
    ospec = pl.BlockSpec((None, CMP_OUT, CMP_W), lambda b, c, pt: (b, c, 0))
    return pl.pallas_call(
        functools.partial(_compress_kernel, layer),
        grid_spec=pltpu.PrefetchScalarGridSpec(
            num_scalar_prefetch=1,
            grid=(B, nc),
            in_specs=[pl.BlockSpec(memory_space=pl.ANY),
                      pl.BlockSpec(memory_space=pl.ANY),
                      pl.BlockSpec(pet.shape, lambda b, c, pt: (0, 0, 0, 0)),
                      pl.BlockSpec(w2bd.shape, lambda b, c, pt: (0, 0, 0))],
            out_specs=[ospec, ospec],
            scratch_shapes=[pltpu.VMEM((2, CMP_PLANES, CMP_ROWS + CMP_STRIDE, LANES), F32),
                            pltpu.VMEM(w1bd.shape, BF16),
                            pltpu.VMEM((CMP_OUT, CMP_BLOCK * LANES), BF16),
                            pltpu.SemaphoreType.DMA((2,)),
                            pltpu.SemaphoreType.DMA(())]),
        out_shape=[out, out],
        compiler_params=_cparams("arbitrary", "arbitrary"),
        name="nsa_compress",
    )(page_table, cache, w1bd, pet, w2bd)


NSA_TQ = 256
NSA_TK = 512
SEL_PER_CMP = SEL_BLOCK // CMP_STRIDE
NT_DIMS = (((1,), (1,)), ((), ()))


def _masked_softmax(s, mask):
    s = jnp.where(mask, s, NEG)
    e = jnp.exp(s - jnp.max(s, axis=-1, keepdims=True)) * mask.astype(F32)
    d = jnp.sum(e, axis=-1, keepdims=True)
    return e / jnp.where(d > 0, d, 1.0)


def _pool_matrix(n_cmp, n_sel):
    i = lax.broadcasted_iota(jnp.int32, (n_cmp, n_sel), 0)
    j = lax.broadcasted_iota(jnp.int32, (n_cmp, n_sel), 1)
    lo = SEL_PER_CMP * j - (CMP_BLOCK // CMP_STRIDE - 1)
    return ((i >= lo) & (i <= SEL_PER_CMP * j + SEL_PER_CMP - 1)).astype(BF16)


def _importance(psum, pool):
    return sum(jnp.dot(t, pool, preferred_element_type=F32) for t in _split3(psum))


def _top_k_mask(score, k):
    lane = lax.broadcasted_iota(jnp.int32, score.shape, 1).astype(F32)
    sel = jnp.zeros(score.shape, F32)
    for _ in range(k):
        m = jnp.max(score, axis=-1, keepdims=True)
        idx = jnp.min(jnp.where(score == m, lane, float(score.shape[-1])), axis=-1, keepdims=True)
        hit = lane == idx
        sel = jnp.where(hit, 1.0, sel)
        score = jnp.where(hit, -jnp.inf, score)
    return sel


def _nsa_prompt_kernel(q_ref, qr_ref, kc_ref, vc_ref, ks_ref, vs_ref, kw_ref, vw_ref, gate_ref, o_ref,
                       sel_ref, m_ref, l_ref, acc_ref, oc_ref, ow_ref):
    tq, tk = NSA_TQ, NSA_TK
    s0 = pl.program_id(1) * tq
    tpos = s0 + lax.broadcasted_iota(jnp.int32, (tq, 1), 0)
    n_cmp = kc_ref.shape[0]
    n_sel = sel_ref.shape[1]

    cidx = lax.broadcasted_iota(jnp.int32, (tq, n_cmp), 1)
    cmask = (cidx * CMP_STRIDE + (CMP_BLOCK - 1) <= tpos) & (cidx < n_cmp - 1)
    psum = jnp.zeros((tq, n_cmp), F32)
    for r in range(NSA_REP):
        s = lax.dot_general(q_ref[r], kc_ref[...], NT_DIMS, preferred_element_type=F32)
        p = _masked_softmax(s, cmask)
        oc_ref[r] = jnp.dot(p.astype(BF16), vc_ref[...], preferred_element_type=F32)
        psum = psum + p

    imp = _importance(psum, _pool_matrix(n_cmp, n_sel))
    blk = lax.broadcasted_iota(jnp.int32, (tq, n_sel), 1)
    valid = blk * SEL_BLOCK <= tpos
    forced = (blk == 0) | (blk == tpos // SEL_BLOCK)
    score = jnp.where(forced, jnp.inf, jnp.where(valid, imp, -jnp.inf))
    sel_ref[...] = _top_k_mask(score, N_SEL).astype(BF16)

    band = WINDOW + tq
    w0 = pl.multiple_of(jnp.maximum(s0 - WINDOW, 0), tq)
    wpos = w0 + lax.broadcasted_iota(jnp.int32, (tq, band), 1)
    wmask = (wpos <= tpos) & (wpos > tpos - WINDOW)
    kw = kw_ref[pl.ds(w0, band), :]
    vw = vw_ref[pl.ds(w0, band), :]
    for r in range(NSA_REP):
        s = lax.dot_general(qr_ref[r], kw, NT_DIMS, preferred_element_type=F32)
        ow_ref[r] = jnp.dot(_masked_softmax(s, wmask).astype(BF16), vw, preferred_element_type=F32)

    m_ref[...] = jnp.full(m_ref.shape, NEG, F32)
    l_ref[...] = jnp.zeros(l_ref.shape, F32)
    acc_ref[...] = jnp.zeros(acc_ref.shape, F32)

    def sweep(kt, carry):
        k0 = pl.multiple_of(kt * tk, tk)
        jj = lax.broadcasted_iota(jnp.int32, (n_sel, tk), 0)
        ss = lax.broadcasted_iota(jnp.int32, (n_sel, tk), 1)
        expand = (jj == kt * (tk // SEL_BLOCK) + ss // SEL_BLOCK).astype(BF16)
        chosen = jnp.dot(sel_ref[...], expand, preferred_element_type=F32)
        kpos = k0 + lax.broadcasted_iota(jnp.int32, (tq, tk), 1)
        allowed = (chosen > 0.5) & (kpos <= tpos)
        ks = ks_ref[pl.ds(k0, tk), :]
        vs = vs_ref[pl.ds(k0, tk), :]
        for r in range(NSA_REP):
            s = lax.dot_general(qr_ref[r], ks, NT_DIMS, preferred_element_type=F32)
            s = jnp.where(allowed, s, NEG)
            m_old = m_ref[r]
            m_new = jnp.maximum(m_old, jnp.max(s, axis=-1, keepdims=True))
            p = jnp.exp(s - m_new)
            alpha = jnp.exp(m_old - m_new)
            l_ref[r] = alpha * l_ref[r] + jnp.sum(p, axis=-1, keepdims=True)
            acc_ref[r] = alpha * acc_ref[r] + jnp.dot(p.astype(BF16), vs, preferred_element_type=F32)
            m_ref[r] = m_new
        return carry

    lax.fori_loop(0, (s0 + tq + tk - 1) // tk, sweep, 0)

    gates = _sigmoid(gate_ref[...])
    for r in range(NSA_REP):
        o_s = acc_ref[r] / l_ref[r]
        o = (gates[:, 3 * r:3 * r + 1] * oc_ref[r] + gates[:, 3 * r + 1:3 * r + 2] * o_s
             + gates[:, 3 * r + 2:3 * r + 3] * ow_ref[r])
        o_ref[:, r * NSA_HD:(r + 1) * NSA_HD] = o.astype(o_ref.dtype)


def nsa_prompt(q, qr, kc, vc, ks, vs, kw, vw, gates):
    L = q.shape[1]
    n_cmp = kc.shape[1]
    n_sel = L // SEL_BLOCK
    assert L % NSA_TK == 0 and L >= WINDOW + NSA_TQ
    qspec = pl.BlockSpec((NSA_REP, NSA_TQ, NSA_HD), lambda g, i: (g, i, 0))
    cspec = pl.BlockSpec((None, n_cmp, NSA_HD), lambda g, i: (g, 0, 0))
    kspec = pl.BlockSpec((None, L, NSA_HD), lambda g, i: (g, 0, 0))
    per_head = lambda w: pltpu.VMEM((NSA_REP, NSA_TQ, w), F32)
    return pl.pallas_call(
        _nsa_prompt_kernel,
        grid=(NSA_KV, L // NSA_TQ),
        in_specs=[qspec, qspec, cspec, cspec, kspec, kspec, kspec, kspec,
                  pl.BlockSpec((None, NSA_TQ, LANES), lambda g, i: (g, i, 0))],
        out_specs=pl.BlockSpec((NSA_TQ, NSA_REP * NSA_HD), lambda g, i: (i, g)),
        out_shape=jax.ShapeDtypeStruct((L, NSA_HEADS * NSA_HD), BF16),
        scratch_shapes=[pltpu.VMEM((NSA_TQ, n_sel), BF16),
                        per_head(1), per_head(1),
                        per_head(NSA_HD), per_head(NSA_HD), per_head(NSA_HD)],
        compiler_params=_cparams("parallel", "arbitrary"),
        name="nsa_prompt",
    )(q, qr, kc, vc, ks, vs, kw, vw, gates)


def _top_k_indices(score, k):
    lane = lax.broadcasted_iota(jnp.int32, score.shape, 1).astype(F32)
    out = jnp.zeros(score.shape, F32)
    for t in range(k):
        m = jnp.max(score, axis=-1, keepdims=True)
        idx = jnp.min(jnp.where(score == m, lane, float(score.shape[-1])), axis=-1, keepdims=True)
        out = jnp.where(lane == float(t), idx, out)
        score = jnp.where(lane == idx, -jnp.inf, score)
    return out.astype(jnp.int32)


def _nsa_sample_select_kernel(q_pos, q_ref, kc_ref, vc_ref, oc_ref, idx_ref):
    n_cmp = kc_ref.shape[0]
    n_sel = idx_ref.shape[-1]
    q = q_ref[...]
    head_group = lax.broadcasted_iota(jnp.int32, (NSA_HEADS, 1), 0) // NSA_REP
    cidx = lax.broadcasted_iota(jnp.int32, (NSA_HEADS, n_cmp), 1)
    cmask = (cidx * CMP_STRIDE + (CMP_BLOCK - 1) <= q_pos) & (cidx < n_cmp - 1)
    grp_row = lax.broadcasted_iota(jnp.int32, (NSA_KV, 1), 0)
    oc = jnp.zeros((NSA_HEADS, NSA_HD), F32)
    psum = jnp.zeros((NSA_KV, n_cmp), F32)
    for g in range(NSA_KV):
        cols = slice(g * NSA_HD, (g + 1) * NSA_HD)
        s = lax.dot_general(q, kc_ref[:, cols].astype(BF16), NT_DIMS, preferred_element_type=F32)
        p = _masked_softmax(s, cmask)
        o = jnp.dot(p.astype(BF16), vc_ref[:, cols].astype(BF16), preferred_element_type=F32)
        mine = head_group == g
        oc = jnp.where(mine, o, oc)
        pg = jnp.sum(jnp.where(mine, p, 0.0), axis=0, keepdims=True)
        psum = jnp.where(grp_row == g, pg, psum)
    oc_ref[...] = oc
    imp = _importance(psum, _pool_matrix(n_cmp, n_sel))
    blk = lax.broadcasted_iota(jnp.int32, (NSA_KV, n_sel), 1)
    score = jnp.where(blk == 0, jnp.inf, imp)
    idx_ref[...] = _top_k_indices(score, N_SEL - 1)


def nsa_sample_select(q, kc, vc, q_pos, n_sel):
    B, n_cmp = kc.shape[0], kc.shape[1]
    cspec = pl.BlockSpec((None, n_cmp, CMP_W), lambda b: (b, 0, 0))
    hspec = pl.BlockSpec((None, NSA_HEADS, NSA_HD), lambda b: (b, 0, 0))
    return pl.pallas_call(
        functools.partial(_nsa_sample_select_kernel, q_pos),
        grid=(B,),
        in_specs=[hspec, cspec, cspec],
        out_specs=[hspec, pl.BlockSpec((None, NSA_KV, n_sel), lambda b: (b, 0, 0))],
        out_shape=[jax.ShapeDtypeStruct((B, NSA_HEADS, NSA_HD), F32),
                   jax.ShapeDtypeStruct((B, NSA_KV, n_sel), jnp.int32)],
        compiler_params=_cparams("parallel"),
        name="nsa_sample_select",
    )(q, kc, vc)


N_PAST_SEL = N_SEL - 1
SEL_K_COL = 2 * CMP_W
SEL_V_COL = 3 * CMP_W
BLOCKS_PER_PAGE = PAGE_SIZE // SEL_BLOCK


def _sel_copies(pt_ref, idx_ref, cache_ref, kbuf, vbuf, sem, layer, n_pages, b, slot):
    out = []
    for g in range(NSA_KV):
        for t in range(N_PAST_SEL):
            blk = idx_ref[(b * NSA_KV + g) * N_SEL + t]
            pg = pt_ref[b * n_pages + blk // BLOCKS_PER_PAGE]
            rows = pl.ds((blk % BLOCKS_PER_PAGE) * SEL_BLOCK, SEL_BLOCK)
            dst = pl.ds(t * SEL_BLOCK, SEL_BLOCK)
            plane = (g // GROUPS_PER_PLANE) * LANES
            out.append(pltpu.make_async_copy(cache_ref.at[layer, pg, rows, pl.ds(SEL_K_COL + plane, LANES)],
                                             kbuf.at[slot, g, dst, :], sem.at[slot]))
            out.append(pltpu.make_async_copy(cache_ref.at[layer, pg, rows, pl.ds(SEL_V_COL + plane, LANES)],
                                             vbuf.at[slot, g, dst, :], sem.at[slot]))
    return out


def _softmax_with_self(s, s_self, mask=None):
    if mask is not None:
        s = jnp.where(mask, s, NEG)
    m = jnp.maximum(jnp.max(s, axis=-1, keepdims=True), s_self)
    e = jnp.exp(s - m)
    if mask is not None:
        e = e * mask.astype(F32)
    e_self = jnp.exp(s_self - m)
    return e, e_self, jnp.sum(e, axis=-1, keepdims=True) + e_self


def _nsa_sample_attend_kernel(layer, n_pages, win_first, pt_ref, idx_ref, cache_ref, qr_ref, new_ref, win_ref, oc_ref,
                              gate_ref, o_ref, kbuf, vbuf, sem):
    b = pl.program_id(0)
    nb = pl.num_programs(0)
    slot = b % 2
    mk = functools.partial(_sel_copies, pt_ref, idx_ref, cache_ref, kbuf, vbuf, sem, layer, n_pages)

    @pl.when(b == 0)
    def _():
        for cp in mk(b, slot):
            cp.start()

    @pl.when(b + 1 < nb)
    def _():
        for cp in mk(b + 1, 1 - slot):
            cp.start()

    for cp in mk(b, slot):
        cp.wait()

    qr = qr_ref[...]
    qf = qr.astype(F32)
    head_group = lax.broadcasted_iota(jnp.int32, (NSA_HEADS, 1), 0) // NSA_REP
    gates = _sigmoid(gate_ref[...])
    n_win = win_ref.shape[0]
    wmask = lax.broadcasted_iota(jnp.int32, (NSA_HEADS, n_win), 1) >= win_first
    out = jnp.zeros((NSA_HEADS, NSA_HD), F32)
    for g in range(NSA_KV):
        cols = slice(g * NSA_HD, (g + 1) * NSA_HD)
        ks_new, vs_new = new_ref[0:1, cols], new_ref[1:2, cols]
        kw_new, vw_new = new_ref[2:3, cols], new_ref[3:4, cols]
        half = slice((g % GROUPS_PER_PLANE) * NSA_HD, (g % GROUPS_PER_PLANE + 1) * NSA_HD)
        s = lax.dot_general(qr, kbuf[slot, g][:, half].astype(BF16), NT_DIMS, preferred_element_type=F32)
        e, e_self, d = _softmax_with_self(s, jnp.sum(qf * ks_new, axis=-1, keepdims=True))
        o_s = (jnp.dot(e.astype(BF16), vbuf[slot, g][:, half].astype(BF16), preferred_element_type=F32)
               + e_self * vs_new) / d
        s = lax.dot_general(qr, win_ref[:, cols].astype(BF16), NT_DIMS, preferred_element_type=F32)
        e, e_self, d = _softmax_with_self(s, jnp.sum(qf * kw_new, axis=-1, keepdims=True), wmask)
        vw = win_ref[:, CMP_W + g * NSA_HD:CMP_W + (g + 1) * NSA_HD].astype(BF16)
        o_w = (jnp.dot(e.astype(BF16), vw, preferred_element_type=F32) + e_self * vw_new) / d
        o = gates[:, 0:1] * oc_ref[...] + gates[:, 1:2] * o_s + gates[:, 2:3] * o_w
        out = jnp.where(head_group == g, o, out)
    o_ref[...] = out


def nsa_sample_attend(cache, layer, page_table, idx, qr, new_rows, win, o_cmp, gates, win_first):
    B, n_pages = page_table.shape
    n_win = win.shape[2]
    hspec = lambda: pl.BlockSpec((None, NSA_HEADS, NSA_HD), lambda b, pt, ix: (b, 0, 0))
    return pl.pallas_call(
        functools.partial(_nsa_sample_attend_kernel, layer, n_pages, win_first),
        grid_spec=pltpu.PrefetchScalarGridSpec(
            num_scalar_prefetch=2,
            grid=(B,),
            in_specs=[pl.BlockSpec(memory_space=pl.ANY),
                      hspec(),
                      pl.BlockSpec((None, 4, CMP_W), lambda b, pt, ix: (b, 0, 0)),
                      pl.BlockSpec((None, None, n_win, 2 * CMP_W), lambda b, pt, ix: (layer, b, 0, 0)),
                      hspec(),
                      pl.BlockSpec((None, NSA_HEADS, LANES), lambda b, pt, ix: (b, 0, 0))],
            out_specs=hspec(),
            scratch_shapes=[pltpu.VMEM((2, NSA_KV, N_PAST_SEL * SEL_BLOCK, LANES), F32),
                            pltpu.VMEM((2, NSA_KV, N_PAST_SEL * SEL_BLOCK, LANES), F32),
                            pltpu.SemaphoreType.DMA((2,))]),
        out_shape=jax.ShapeDtypeStruct((B, NSA_HEADS, NSA_HD), F32),
        compiler_params=_cparams("arbitrary"),
        name="nsa_sample_attend",
    )(page_table.reshape(-1), idx.reshape(-1), cache, qr, new_rows, win, o_cmp, gates)


def _merge_kernel(oa_ref, ob_ref, wa_ref, wb_ref, ma_ref, mb_ref, o_ref, wa_bf, wb_bf):
    @pl.when(pl.program_id(1) == 0)
    def _():
        wa_bf[...] = wa_ref[...].astype(BF16)
        wb_bf[...] = wb_ref[...].astype(BF16)

    ya = jnp.dot(oa_ref[...], wa_bf[...], preferred_element_type=F32)
    yb = jnp.dot(ob_ref[...], wb_bf[...], preferred_element_type=F32)
    o_ref[...] = (_sigmoid(ma_ref[...]) * ya + _sigmoid(mb_ref[...]) * yb).astype(o_ref.dtype)


def merge(o_a, o_b, w_a, w_b, zm, layer, *, tm, tn=512):
    M, K = o_a.shape
    D = w_a.shape[-1]
    xspec = pl.BlockSpec((tm, K), lambda j, i: (i, 0))
    wspec = pl.BlockSpec((None, K, tn), lambda j, i: (layer, 0, j))
    return pl.pallas_call(
        _merge_kernel,
        grid=(D // tn, M // tm),
        in_specs=[xspec, xspec, wspec, wspec,
                  pl.BlockSpec((tm, tn), lambda j, i: (i, j)),
                  pl.BlockSpec((tm, tn), lambda j, i: (i, j + D // tn))],
        out_specs=pl.BlockSpec((tm, tn), lambda j, i: (i, j)),
        out_shape=jax.ShapeDtypeStruct((M, D), BF16),
        scratch_shapes=[pltpu.VMEM((K, tn), BF16), pltpu.VMEM((K, tn), BF16)],
        compiler_params=_cparams("arbitrary", "arbitrary"),
        name="merge",
    )(o_a, o_b, w_a, w_b, zm, zm)


MOE_TILE = 256
ROUTE_TILE = 256


def _first_argmax(vals):
    best, idx = vals[0], jnp.zeros(vals[0].shape, F32)
    for e in range(1, len(vals)):
        better = vals[e] > best
        idx = jnp.where(better, float(e), idx)
        best = jnp.where(better, vals[e], best)
    return best, idx


def _router_kernel(x_ref, w_ref, b_ref, o_ref):
    xs = _split3(x_ref[...])
    ws = _split3(w_ref[...])
    terms = ((0, 0), (0, 1), (1, 0), (0, 2), (2, 0), (1, 1))
    logits = b_ref[...] + sum(lax.dot_general(ws[a], xs[c], NT_DIMS, preferred_element_type=F32) for a, c in terms)
    m = jnp.max(logits, axis=0, keepdims=True)
    e = jnp.exp(logits - m)
    probs = e / jnp.sum(e, axis=0, keepdims=True)
    p = [probs[i:i + 1] for i in range(N_EXPERTS)]
    gscore = []
    for g in range(N_GROUPS):
        mem = p[g * EXPERTS_PER_GROUP:(g + 1) * EXPERTS_PER_GROUP]
        pairs = [mem[a] + mem[c] for a in range(EXPERTS_PER_GROUP) for c in range(a + 1, EXPERTS_PER_GROUP)]
        gscore.append(functools.reduce(jnp.maximum, pairs))
    _, g_sel = _first_argmax(gscore)
    masked = [jnp.where(g_sel == float(i // EXPERTS_PER_GROUP), p[i], -1.0) for i in range(N_EXPERTS)]
    v1, i1 = _first_argmax(masked)
    masked2 = [jnp.where(i1 == float(i), -2.0, masked[i]) for i in range(N_EXPERTS)]
    v2, i2 = _first_argmax(masked2)
    tot = v1 + v2
    o_ref[...] = jnp.concatenate([i1, i2, v1 / tot, v2 / tot, jnp.zeros((4,) + i1.shape[1:], F32)], axis=0)


def moe_route(x, w_router, b_router):
    T, D = x.shape
    return pl.pallas_call(
        _router_kernel,
        grid=(T // ROUTE_TILE,),
        in_specs=[pl.BlockSpec((ROUTE_TILE, D), lambda i: (i, 0)),
                  pl.BlockSpec((N_EXPERTS, D), lambda i: (0, 0)),
                  pl.BlockSpec((N_EXPERTS, 1), lambda i: (0, 0))],
        out_specs=pl.BlockSpec((8, ROUTE_TILE), lambda i: (0, i)),
        out_shape=jax.ShapeDtypeStruct((8, T), F32),
        compiler_params=_cparams("parallel"),
        name="moe_route",
    )(x, w_router.T, b_router.reshape(N_EXPERTS, 1))


def _start_row_copies(n, hbm, rows_ref, base, buf, sub, sem, to_hbm):
    def body(j, carry):
        win = buf.at[pl.ds(pl.multiple_of(j * sub, sub), sub), :]
        row = hbm.at[pl.ds(pl.multiple_of(rows_ref[base + j] * sub, sub), sub), :]
        (pltpu.make_async_copy(win, row, sem) if to_hbm else pltpu.make_async_copy(row, win, sem)).start()
        return carry

    lax.fori_loop(0, n, body, 0)


def _wait_row_copies(n, hbm, buf, sub, sem, to_hbm):
    size = MOE_TILE
    while size:
        def wait(size=size):
            a, b = buf.at[pl.ds(0, size * sub), :], hbm.at[pl.ds(0, size * sub), :]
            (pltpu.make_async_copy(a, b, sem) if to_hbm else pltpu.make_async_copy(b, a, sem)).wait()

        if isinstance(n, int):
            if n & size:
                wait()
        else:
            pl.when((n & size) != 0)(wait)
        size //= 2


def _expert_ffn_kernel(src_ref, dst_ref, te_ref, nv_ref, nt_ref, xw_hbm, w1_ref, w3_ref, w2_ref, y_hbm,
                       xbuf, x2d, y2d, obuf, gsem, ssem):
    t = pl.program_id(0)
    nt = nt_ref[0]
    slot = t % 2
    xsub = xbuf.shape[1] // MOE_TILE
    ysub = obuf.shape[0] // MOE_TILE
    start_gather = lambda tile, sl: _start_row_copies(MOE_TILE, xw_hbm, src_ref, tile * MOE_TILE, xbuf.at[sl], xsub,
                                                      gsem.at[sl], False)
    wait_scatter = lambda tile: _wait_row_copies(nv_ref[tile], y_hbm, obuf, ysub, ssem, True)

    @pl.when((t == 0) & (nt > 0))
    def _():
        start_gather(t, slot)

    @pl.when(t + 1 < nt)
    def _():
        start_gather(t + 1, 1 - slot)

    @pl.when(t < nt)
    def _():
        _wait_row_copies(MOE_TILE, xw_hbm, xbuf.at[slot], xsub, gsem.at[slot], False)
        xs = xbuf.at[slot]
        for r in range(MOE_TILE // SUBLANES):
            for c in range(xsub):
                x2d[r * SUBLANES:(r + 1) * SUBLANES, c * LANES:(c + 1) * LANES] = (
                    xs[pl.ds(r * SUBLANES * xsub + c, SUBLANES, stride=xsub), :])
        x = x2d[...].astype(BF16)
        h = _silu(jnp.dot(x, w1_ref[...], preferred_element_type=F32)) * jnp.dot(x, w3_ref[...],
                                                                              preferred_element_type=F32)
        y2d[...] = jnp.dot(h.astype(BF16), w2_ref[...], preferred_element_type=F32)

        @pl.when(t > 0)
        def _():
            wait_scatter(t - 1)

        for r in range(MOE_TILE // SUBLANES):
            for c in range(ysub):
                obuf[pl.ds(r * SUBLANES * ysub + c, SUBLANES, stride=ysub), :] = (
                    y2d[r * SUBLANES:(r + 1) * SUBLANES, c * LANES:(c + 1) * LANES])
        _start_row_copies(nv_ref[t], y_hbm, dst_ref, t * MOE_TILE, obuf, ysub, ssem, True)

        @pl.when(t == nt - 1)
        def _():
            wait_scatter(t)


def expert_ffn(xw, src_token, dst_row, n_valid, n_rows_out, tile_expert, n_tiles_used, w1, w3, w2, layer):
    P = src_token.shape[0]
    D, F = w1.shape[2], w1.shape[3]
    xsub = ysub = D // LANES
    wspec = lambda a, c: pl.BlockSpec((None, None, a, c), lambda t, src, dst, te, nv, nt: (layer, te[t], 0, 0))
    return pl.pallas_call(
        _expert_ffn_kernel,
        grid_spec=pltpu.PrefetchScalarGridSpec(
            num_scalar_prefetch=5,
            grid=(P // MOE_TILE,),
            in_specs=[pl.BlockSpec(memory_space=pl.ANY), wspec(D, F), wspec(D, F), wspec(F, D)],
            out_specs=pl.BlockSpec(memory_space=pl.ANY),
            scratch_shapes=[pltpu.VMEM((2, MOE_TILE * xsub, LANES), F32),
                            pltpu.VMEM((MOE_TILE, xsub * LANES), F32),
                            pltpu.VMEM((MOE_TILE, D), F32),
                            pltpu.VMEM((MOE_TILE * ysub, LANES), F32),
                            pltpu.SemaphoreType.DMA((2,)),
                            pltpu.SemaphoreType.DMA(())]),
        out_shape=jax.ShapeDtypeStruct((n_rows_out * ysub, LANES), F32),
        compiler_params=_cparams("arbitrary"),
        name="expert_ffn",
    )(src_token, dst_row, tile_expert, n_valid, n_tiles_used, xw, w1, w3, w2)


def _combine_ln_kernel(alpha, x_ref, y1_ref, y2_ref, wt_ref, g_ref, b_ref, o_ref, ob_ref, y_scr):
    tm, D = o_ref.shape
    sub = D // LANES
    for r in range(tm // SUBLANES):
        rows = slice(r * SUBLANES, (r + 1) * SUBLANES)
        w1, w2 = wt_ref[rows, 0:1], wt_ref[rows, 1:2]
        for c in range(sub):
            pick = pl.ds(r * SUBLANES * sub + c, SUBLANES, stride=sub)
            y_scr[rows, c * LANES:(c + 1) * LANES] = w1 * y1_ref[pick, :] + w2 * y2_ref[pick, :]
    out = _layer_norm(alpha * x_ref[...] + y_scr[...], g_ref[...], b_ref[...])
    o_ref[...] = out
    ob_ref[...] = out.astype(BF16)


def moe_combine_layer_norm(x, ysel, wts, g, b, alpha, *, tm=256):
    T, D = x.shape
    nt = T // tm
    sub = D // LANES
    spec = pl.BlockSpec((tm, D), lambda i: (i, 0))
    vec = pl.BlockSpec((1, D), lambda i: (0, 0))
    return pl.pallas_call(
        functools.partial(_combine_ln_kernel, alpha),
        grid=(nt,),
        in_specs=[spec, pl.BlockSpec((tm * sub, LANES), lambda i: (i, 0)),
                  pl.BlockSpec((tm * sub, LANES), lambda i: (i + nt, 0)),
                  pl.BlockSpec((tm, 2), lambda i: (i, 0)), vec, vec],
        out_specs=[spec, spec],
        out_shape=[jax.ShapeDtypeStruct((T, D), F32), jax.ShapeDtypeStruct((T, D), BF16)],
        scratch_shapes=[pltpu.VMEM((tm, D), F32)],
        compiler_params=_cparams("parallel"),
        name="moe_combine_ln",
    )(x, ysel, ysel, wts, g.reshape(1, D), b.reshape(1, D))


def moe_dispatch_plan(e1, e2, n_tiles):
    T = e1.shape[0]
    ea = jnp.concatenate([e1, e2])
    onehot = (ea[:, None] == jnp.arange(N_EXPERTS)[None, :]).astype(jnp.int32)
    rank = jnp.take_along_axis(jnp.cumsum(onehot, axis=0) - onehot, ea[:, None], axis=1)[:, 0]
    counts = onehot.sum(0)
    tiles = (counts + MOE_TILE - 1) // MOE_TILE
    tile_end = jnp.cumsum(tiles)
    dest = ((tile_end - tiles) * MOE_TILE)[ea] + rank
    n_slots = n_tiles * MOE_TILE
    src_token = jnp.zeros((n_slots,), jnp.int32).at[dest].set(jnp.tile(jnp.arange(T, dtype=jnp.int32), 2))
    dst_row = jnp.zeros((n_slots,), jnp.int32).at[dest].set(jnp.arange(2 * T, dtype=jnp.int32))
    tile_idx = jnp.arange(n_tiles)
    tile_expert = jnp.minimum((tile_end[None, :] <= tile_idx[:, None]).sum(-1), N_EXPERTS - 1)
    first_tile = (tile_end - tiles)[tile_expert]
    n_valid = jnp.clip(counts[tile_expert] - (tile_idx - first_tile) * MOE_TILE, 0, MOE_TILE)
    return (src_token, dst_row, tile_expert.astype(jnp.int32), n_valid.astype(jnp.int32),
            tile_end[-1:].astype(jnp.int32))


def moe_block(x, x_slabs, w_router, b_router, w1, w3, w2, layer, g, b, alpha):
    T, D = x.shape
    r = moe_route(x, w_router, b_router)
    e1, e2 = r[0].astype(jnp.int32), r[1].astype(jnp.int32)
    n_tiles = 2 * T // MOE_TILE + N_EXPERTS
    src_token, dst_row, tile_expert, n_valid, n_used = moe_dispatch_plan(e1, e2, n_tiles)
    ysel = expert_ffn(x_slabs, src_token, dst_row, n_valid, 2 * T, tile_expert, n_used, w1, w3, w2, layer)
    return moe_combine_layer_norm(x, ysel, r[2:4].T, g, b, alpha)


def _hgrn_sample_kernel(qt_ref, ft_ref, lbt_ref, onemt_ref, ia_ref, ga_ref, gn_ref, s0_ref, o_ref, s_ref):
    qa = qt_ref[...]
    fa = ft_ref[...]
    q_all = _silu(qa)
    sig = _sigmoid(fa)
    f_all = lbt_ref[...] + onemt_ref[...] * sig
    k_all = onemt_ref[...] * _sigmoid(-fa)
    for h in range(A_HEADS):
        cols = slice(h * A_DK, (h + 1) * A_DK)
        v = ia_ref[:, cols]
        s_new = f_all[:, h:h + 1] * s0_ref[h] + k_all[:, h:h + 1] * v
        s_ref[h] = s_new
        o = jnp.sum(s_new * q_all[:, h:h + 1], axis=0, keepdims=True)
        o_ref[:, cols] = _hgrn_gate_norm(o, ga_ref[:, cols], gn_ref[:, cols]).astype(o_ref.dtype)


def hgrn_sample(qa_t, fa_t, lb_t, onem_t, ia, ga, gnorm, state, layer):
    B = qa_t.shape[0]
    W = A_HEADS * A_DK
    col = pl.BlockSpec((None, A_DK, A_HEADS), lambda b: (b, 0, 0))
    par = pl.BlockSpec((A_DK, A_HEADS), lambda b: (0, 0))
    row = pl.BlockSpec((None, 1, W), lambda b: (b, 0, 0))
    return pl.pallas_call(
        _hgrn_sample_kernel,
        grid=(B,),
        in_specs=[col, col, par, par, row, row, pl.BlockSpec((1, W), lambda b: (0, 0)),
                  pl.BlockSpec((None, None, A_HEADS, A_DK, A_DK), lambda b: (layer, b, 0, 0, 0))],
        out_specs=[row, pl.BlockSpec((None, A_HEADS, A_DK, A_DK), lambda b: (b, 0, 0, 0))],
        out_shape=[jax.ShapeDtypeStruct((B, 1, W), BF16), jax.ShapeDtypeStruct((B, A_HEADS, A_DK, A_DK), F32)],
        compiler_params=_cparams("parallel"),
        name="hgrn_sample",
    )(qa_t, fa_t, lb_t, onem_t, ia, ga, gnorm, state)


MM_TILE_M = 768
MM_TILE_N = 512
W_A = A_HEADS * A_DK
W_Q = NSA_HEADS * NSA_HD
COL_QB = 4 * W_A
COL_KV = COL_QB + W_Q
COL_GB = COL_KV + 6 * CMP_W
COL_MA = COL_GB + 3 * NSA_HEADS


def _pad_lanes(a):
    return jnp.pad(a, [(0, 0)] * (a.ndim - 1) + [(0, LANES - a.shape[-1])])

def kernel(x_prompt, x_sample, cache_nsa_kv, cache_nsa_win, state_hgrn, page_table, w_in, lb_raw, gnorm_a, cmp_pe, cmp_w1, cmp_w2, w_branch_a, w_branch_b, w_out, ln1_g, ln1_b, ln2_g, ln2_b, w_router, b_router, w_e1, w_e3, w_e2):
    depth, D = w_in.shape[0], w_in.shape[1]
    L, B = x_prompt.shape[1], x_sample.shape[0]
    assert x_prompt.shape[0] == 1 and x_sample.shape[1] == 1
    n_pool = cache_nsa_kv.shape[1]
    past_len = page_table.shape[1] * PAGE_SIZE
    win_rows = cache_nsa_win.shape[2]
    T = L + B
    M = -(-T // MM_TILE_M) * MM_TILE_M
    alpha = (2 * depth) ** 0.25
    prompt, sample = slice(0, L), slice(L, T)

    lbs = jnp.cumsum(jax.nn.softmax(lb_raw.astype(F32), axis=0), axis=0)
    lbs = lbs - lbs[0:1]
    pos = jnp.concatenate([jnp.arange(L), jnp.full((B,), past_len), jnp.zeros((M - T,), jnp.int32)])
    cos, sin = rope_tables(pos)
    ones, zeros = jnp.ones_like(cos), jnp.zeros_like(cos)
    cache = cache_nsa_kv.reshape(depth, n_pool, PAGE_SIZE, 4 * CMP_W)
    win_cache = cache_nsa_win.reshape(depth, B, win_rows, 2 * CMP_W)
    w1b, w3b, w2b = w_e1.astype(BF16), w_e3.astype(BF16), w_e2.astype(BF16)
    scale = NSA_HD ** -0.5

    x = jnp.concatenate([x_prompt[0], x_sample[:, 0], jnp.zeros((M - T, D), F32)])
    xb = x.astype(BF16)
    heads_major = lambda a, h: a.reshape(a.shape[0], h, NSA_HD).transpose(1, 0, 2)
    col_major = lambda a: a.reshape(B, A_HEADS, A_DK).transpose(0, 2, 1)
    outs = [[] for _ in range(6)]
    for l in range(depth):
        lb = lbs[l]
        mm = functools.partial(matmul, tm=MM_TILE_M, tn=MM_TILE_N)
        z = mm(xb, w_in, l, col0=0, ncols=COL_GB)
        zg = matmul(xb, _pad_lanes(w_in[l, :, COL_GB:COL_MA])[None], 0, tm=MM_TILE_M, tn=LANES)
        zm = mm(xb, w_in[l, :, COL_MA:][None], 0)

        o_a_p, st_p = hgrn_prompt(z, L, jnp.log(lb)[None], jnp.log1p(-lb)[None], (1.0 - lb)[None], gnorm_a[l][None])
        zs = z[sample]
        o_a_s, st_s = hgrn_sample(col_major(zs[:, :W_A]), col_major(zs[:, W_A:2 * W_A]),
                                  lb.reshape(A_HEADS, A_DK).T, (1.0 - lb).reshape(A_HEADS, A_DK).T,
                                  zs[:, None, 2 * W_A:3 * W_A], zs[:, None, 3 * W_A:4 * W_A], gnorm_a[l][None],
                                  state_hgrn, l)
        o_a = jnp.concatenate([o_a_p, o_a_s[:, 0], jnp.zeros((M - T, W_A), BF16)])

        q_plain = rope(z, COL_QB, W_Q, ones, zeros, scale=scale, out_dtype=BF16)
        q_rot = rope(z, COL_QB, W_Q, cos, sin, scale=scale, out_dtype=BF16)
        ks_rot = rope(z, COL_KV + 2 * CMP_W, CMP_W, cos, sin)
        kw_rot = rope(z, COL_KV + 4 * CMP_W, CMP_W, cos, sin)
        v_sel = z[:, COL_KV + 3 * CMP_W:COL_KV + 4 * CMP_W]
        v_win = z[:, COL_KV + 5 * CMP_W:COL_KV + 6 * CMP_W]
        rows = jnp.concatenate([z[:, COL_KV:COL_KV + 2 * CMP_W], ks_rot, v_sel], axis=-1)
        win_new = jnp.concatenate([kw_rot, v_win], axis=-1)
        cw = compress_weights(cmp_pe[l], cmp_w1[l], cmp_w2[l])

        kc_p, vc_p = compress_paged(rows[prompt].reshape(1, L // PAGE_SIZE, PAGE_SIZE, 4 * CMP_W), 0,
                                    jnp.arange(L // PAGE_SIZE, dtype=jnp.int32)[None], *cw)
        grp = lambda a: heads_major(a, NSA_KV).astype(BF16)
        gates_p =_pad_lanes(zg[prompt, :3 * NSA_HEADS].reshape(L, NSA_KV, 3 * NSA_REP).transpose(1, 0, 2))
        o_b_p = nsa_prompt(heads_major(q_plain[prompt], NSA_HEADS), heads_major(q_rot[prompt], NSA_HEADS),
                           grp(kc_p[0]), grp(vc_p[0]), grp(ks_rot[prompt]), grp(v_sel[prompt]),
                           grp(kw_rot[prompt]), grp(v_win[prompt]), gates_p)

        kc_s, vc_s = compress_paged(cache, l, page_table, *cw)
        o_c_s, idx = nsa_sample_select(q_plain[sample].reshape(B, NSA_HEADS, NSA_HD), kc_s, vc_s,
                                       past_len, past_len // SEL_BLOCK)
        rs, ws = rows[sample], win_new[sample]
        new_rows = jnp.stack([rs[:, 2 * CMP_W:3 * CMP_W], rs[:, 3 * CMP_W:], ws[:, :CMP_W], ws[:, CMP_W:]], axis=1)
        gates_s = _pad_lanes(zg[sample, :3 * NSA_HEADS].reshape(B, NSA_HEADS, 3))
        o_b_s = nsa_sample_attend(cache, l, page_table, idx[:, :, :N_SEL], q_rot[sample].reshape(B, NSA_HEADS, NSA_HD),
                                  new_rows, win_cache, o_c_s, gates_s, win_rows - WINDOW + 1)
        o_b = jnp.concatenate([o_b_p, o_b_s.reshape(B, W_Q).astype(BF16), jnp.zeros((M - T, W_Q), BF16)])

        h = merge(o_a, o_b, w_branch_a, w_branch_b, zm, l, tm=MM_TILE_M)
        y = mm(h, w_out, l)
        x1, x1_slabs = residual_layer_norm(x, y, ln1_g[l], ln1_b[l], alpha)
        x, xb = moe_block(x1, x1_slabs, w_router, b_router, w1b, w3b, w2b, l, ln2_g[l], ln2_b[l], alpha)

        wbp = min(WINDOW, L)
        outs[0].append(rows[prompt].reshape(1, L, 4, NSA_KV, NSA_HD))
        outs[1].append(rs.reshape(B, 1, 4, NSA_KV, NSA_HD))
        outs[2].append(win_new[L - wbp:L].reshape(1, wbp, 2, NSA_KV, NSA_HD))
        outs[3].append(jnp.concatenate([cache_nsa_win[l][:, 1:], ws.reshape(B, 1, 2, NSA_KV, NSA_HD)], axis=1))
        outs[4].append(st_p[None])
        outs[5].append(st_s)
    return (x[prompt][None], x[sample][:, None]) + tuple(jnp.stack(o) for o in outs)
```

```python
import functools
import math

import jax
import jax.numpy as jnp
import numpy as np
from jax import lax
from jax.experimental import pallas as pl
from jax.experimental.pallas import tpu as pltpu

F32 = jnp.float32
BF16 = jnp.bfloat16

A_HEADS = 8
A_DK = 128
NSA_HEADS = 16
NSA_KV = 4
NSA_REP = NSA_HEADS // NSA_KV
NSA_HD = 64
CMP_BLOCK = 32
CMP_STRIDE = 16
CMP_HIDDEN = 128
SEL_BLOCK = 64
N_SEL = 16
WINDOW = 512
PAGE_SIZE = 128
ROPE_THETA = 10000.0
N_EXPERTS = 16
N_GROUPS = 4
EXPERTS_PER_GROUP = N_EXPERTS // N_GROUPS
LN_EPS = 1e-5
NEG = -1e30

VMEM_LIMIT_BYTES = 56 * 1024 * 1024
LANES = 128


def _cparams(*sem):
    return pltpu.CompilerParams(dimension_semantics=sem, vmem_limit_bytes=VMEM_LIMIT_BYTES)


def _sigmoid(x):
    return 1.0 / (1.0 + jnp.exp(-x))


def _silu(x):
    return x * _sigmoid(x)


def _mm_kernel(x_ref, w_ref, o_ref, wb_ref):
    @pl.when(pl.program_id(1) == 0)
    def _():
        wb_ref[...] = w_ref[...].astype(BF16)

    o_ref[...] = jnp.dot(x_ref[...], wb_ref[...], preferred_element_type=F32).astype(o_ref.dtype)


def matmul(x, w, layer, *, tm, tn, col0=0, ncols=None, out_dtype=F32):
    M, K = x.shape
    ncols = w.shape[-1] if ncols is None else ncols
    assert M % tm == 0 and ncols % tn == 0 and col0 % tn == 0
    off = col0 // tn
    return pl.pallas_call(
        _mm_kernel,
        grid=(ncols // tn, M // tm),
        in_specs=[pl.BlockSpec((tm, K), lambda j, i: (i, 0)),
                  pl.BlockSpec((None, K, tn), lambda j, i: (layer, 0, j + off))],
        out_specs=pl.BlockSpec((tm, tn), lambda j, i: (i, j)),
        out_shape=jax.ShapeDtypeStruct((M, ncols), out_dtype),
        scratch_shapes=[pltpu.VMEM((K, tn), BF16)],
        compiler_params=_cparams("arbitrary", "arbitrary"),
        name="matmul",
    )(x, w)


SUBLANES = 8


def _layer_norm(v, g, b):
    mu = jnp.mean(v, axis=-1, keepdims=True)
    d = v - mu
    var = jnp.mean(d * d, axis=-1, keepdims=True)
    return d * lax.rsqrt(var + LN_EPS) * g + b


def _ln_kernel(alpha, x_ref, y_ref, g_ref, b_ref, o_ref, slab_ref):
    o_ref[...] = _layer_norm(alpha * x_ref[...] + y_ref[...], g_ref[...], b_ref[...])
    tm, D = o_ref.shape
    sub = D // LANES
    for r in range(tm // SUBLANES):
        for c in range(sub):
            slab_ref[pl.ds(r * SUBLANES * sub + c, SUBLANES, stride=sub), :] = (
                o_ref[r * SUBLANES:(r + 1) * SUBLANES, c * LANES:(c + 1) * LANES])


def residual_layer_norm(x, y, g, b, alpha, *, tm=256):
    M, D = x.shape
    spec = pl.BlockSpec((tm, D), lambda i: (i, 0))
    vec = pl.BlockSpec((1, D), lambda i: (0, 0))
    return pl.pallas_call(
        functools.partial(_ln_kernel, alpha),
        grid=(M // tm,),
        in_specs=[spec, spec, vec, vec],
        out_specs=[spec, pl.BlockSpec((tm * D // LANES, LANES), lambda i: (i, 0))],
        out_shape=[jax.ShapeDtypeStruct((M, D), F32), jax.ShapeDtypeStruct((M * D // LANES, LANES), F32)],
        compiler_params=_cparams("parallel"),
        name="residual_ln",
    )(x, y, g.reshape(1, D), b.reshape(1, D))


def _rope_kernel(scale, x_ref, cos_ref, sin_ref, o_ref):
    cos = cos_ref[...]
    sin = sin_ref[...]
    lane = lax.broadcasted_iota(jnp.int32, cos.shape, 1)
    first_half = (lane % NSA_HD) < (NSA_HD // 2)
    for j in range(x_ref.shape[1] // LANES):
        x = x_ref[:, j * LANES:(j + 1) * LANES]
        partner = jnp.where(first_half, pltpu.roll(x, LANES - NSA_HD // 2, 1), pltpu.roll(x, NSA_HD // 2, 1))
        o_ref[:, j * LANES:(j + 1) * LANES] = ((x * cos + partner * sin) * scale).astype(o_ref.dtype)


def rope(z, col0, width, cos, sin, *, scale=1.0, out_dtype=F32, tm=256):
    M = z.shape[0]
    assert col0 % width == 0 and width % LANES == 0
    return pl.pallas_call(
        functools.partial(_rope_kernel, scale),
        grid=(M // tm,),
        in_specs=[pl.BlockSpec((tm, width), lambda i: (i, col0 // width)),
                  pl.BlockSpec((tm, LANES), lambda i: (i, 0)),
                  pl.BlockSpec((tm, LANES), lambda i: (i, 0))],
        out_specs=pl.BlockSpec((tm, width), lambda i: (i, 0)),
        out_shape=jax.ShapeDtypeStruct((M, width), out_dtype),
        compiler_params=_cparams("parallel"),
        name="rope",
    )(z, cos, sin)


def rope_tables(pos):
    half = NSA_HD // 2
    inv = jnp.power(ROPE_THETA, -jnp.arange(half, dtype=F32) / half)
    ang = pos.astype(F32)[:, None] * inv[None, :]
    cos, sin = jnp.cos(ang), jnp.sin(ang)
    reps = LANES // NSA_HD
    return jnp.tile(jnp.concatenate([cos, cos], -1), (1, reps)), jnp.tile(jnp.concatenate([-sin, sin], -1), (1, reps))


HG_CHUNK = 64
HG_SUB = 16
HG_BLOCK = 256


def _split3(x):
    hi = x.astype(BF16)
    r1 = x - hi.astype(F32)
    mid = r1.astype(BF16)
    lo = (r1 - mid.astype(F32)).astype(BF16)
    return hi, mid, lo


def _log_forget(fa, loglb, log1mlb):
    log_sig = jnp.minimum(fa, 0.0) - jnp.log(1.0 + jnp.exp(-jnp.abs(fa)))
    b = log1mlb + log_sig
    mx = jnp.maximum(loglb, b)
    return mx + jnp.log(1.0 + jnp.exp(-jnp.abs(loglb - b)))


def _hgrn_gate_norm(o, ga, gn):
    o = o * lax.rsqrt(jnp.mean(o * o, axis=-1, keepdims=True) + 1e-6)
    return o * gn * _silu(ga)


def _hgrn_prompt_kernel(qa_ref, fa_ref, ia_ref, ga_ref, loglb_ref, log1mlb_ref, onemlb_ref, gn_ref,
                        o_ref, s_ref, st_ref):
    c = pl.program_id(1)

    @pl.when(c == 0)
    def _():
        st_ref[...] = jnp.zeros_like(st_ref)

    loglb, log1mlb, onemlb, gn = loglb_ref[...], log1mlb_ref[...], onemlb_ref[...], gn_ref[...]
    row = lax.broadcasted_iota(jnp.int32, (HG_CHUNK, HG_CHUNK), 0)
    col = lax.broadcasted_iota(jnp.int32, (HG_CHUNK, HG_CHUNK), 1)
    tril = (col <= row).astype(BF16)
    srow = lax.broadcasted_iota(jnp.int32, (HG_SUB, HG_SUB), 0)
    scol = lax.broadcasted_iota(jnp.int32, (HG_SUB, HG_SUB), 1)

    def chunk(ci, carry):
        r0 = pl.multiple_of(ci * HG_CHUNK, HG_CHUNK)
        rows = pl.ds(r0, HG_CHUNK)
        qa, fa, v = qa_ref[rows, :], fa_ref[rows, :], ia_ref[rows, :]
        q = _silu(qa)
        logf = _log_forget(fa, loglb, log1mlb)
        k = onemlb * _sigmoid(-fa)
        b = sum(jnp.dot(tril, t, preferred_element_type=F32) for t in _split3(logf))
        st = st_ref[...]
        vb = v.astype(BF16)
        o_inter = lax.dot_general((q * jnp.exp(b)).astype(BF16), st.astype(BF16), (((1,), (1,)), ((), ())),
                                  preferred_element_type=F32)
        for i in range(HG_CHUNK // HG_SUB):
            lo = i * HG_SUB
            qi, bi, ki = q[lo:lo + HG_SUB], b[lo:lo + HG_SUB], k[lo:lo + HG_SUB]
            o = o_inter[lo:lo + HG_SUB]
            if i > 0:
                anchor = b[lo - 1:lo]
                qt = (qi * jnp.exp(bi - anchor)).astype(BF16)
                kt = (k[:lo] * jnp.exp(anchor - b[:lo])).astype(BF16)
                a_off = lax.dot_general(qt, kt, (((1,), (1,)), ((), ())), preferred_element_type=F32)
                o = o + jnp.dot(a_off.astype(BF16), vb[:lo], preferred_element_type=F32)
            a_d = jnp.zeros((HG_SUB, HG_SUB), F32)
            for s in range(HG_SUB):
                p = qi * ki[s:s + 1] * jnp.exp(jnp.minimum(bi - bi[s:s + 1], 0.0))
                a_d = jnp.where(scol == s, jnp.sum(p, axis=-1, keepdims=True), a_d)
            a_d = jnp.where(scol <= srow, a_d, 0.0)
            o = o + jnp.dot(a_d.astype(BF16), vb[lo:lo + HG_SUB], preferred_element_type=F32)
            ga = ga_ref[pl.ds(r0 + lo, HG_SUB), :]
            o_ref[pl.ds(r0 + lo, HG_SUB), :] = _hgrn_gate_norm(o, ga, gn).astype(o_ref.dtype)
        b_last = b[HG_CHUNK - 1:HG_CHUNK]
        kt = (k * jnp.exp(b_last - b)).astype(BF16)
        st_ref[...] = st * jnp.exp(b_last) + lax.dot_general(vb, kt, (((0,), (0,)), ((), ())),
                                                             preferred_element_type=F32)
        return carry

    lax.fori_loop(0, HG_BLOCK // HG_CHUNK, chunk, 0, unroll=True)

    @pl.when(c == pl.num_programs(1) - 1)
    def _():
        s_ref[...] = st_ref[...].T


def hgrn_prompt(z, L, loglb, log1mlb, onemlb, gnorm):
    blk = lambda part: pl.BlockSpec((HG_BLOCK, A_DK), lambda h, c: (c, part * A_HEADS + h))
    vec = pl.BlockSpec((1, A_DK), lambda h, c: (0, h))
    return pl.pallas_call(
        _hgrn_prompt_kernel,
        grid=(A_HEADS, L // HG_BLOCK),
        in_specs=[blk(0), blk(1), blk(2), blk(3), vec, vec, vec, vec],
        out_specs=[pl.BlockSpec((HG_BLOCK, A_DK), lambda h, c: (c, h)),
                   pl.BlockSpec((None, A_DK, A_DK), lambda h, c: (h, 0, 0))],
        out_shape=[jax.ShapeDtypeStruct((L, A_HEADS * A_DK), BF16),
                   jax.ShapeDtypeStruct((A_HEADS, A_DK, A_DK), F32)],
        scratch_shapes=[pltpu.VMEM((A_DK, A_DK), F32)],
        compiler_params=_cparams("arbitrary", "arbitrary"),
        name="hgrn_prompt",
    )(z, z, z, z, loglb, log1mlb, onemlb, gnorm)


CMP_PAGES = 32
CMP_ROWS = CMP_PAGES * PAGE_SIZE
CMP_OUT = CMP_ROWS // CMP_STRIDE
CMP_W = NSA_KV * NSA_HD
CMP_PLANES = 2 * CMP_W // LANES
GROUPS_PER_PLANE = LANES // NSA_HD


def _compress_copies(pt_ref, cache_ref, tbuf, sem, layer, bb, cc, slot, n_pages):
    nxt = jnp.minimum((cc + 1) * CMP_PAGES, n_pages - 1)
    pages = [pt_ref[bb, cc * CMP_PAGES + j] for j in range(CMP_PAGES)] + [pt_ref[bb, nxt]]
    return [pltpu.make_async_copy(cache_ref.at[layer, pg, pl.ds(0, 2)], tbuf.at[slot, j], sem.at[slot])
            for j, pg in enumerate(pages)]


def _compress_kernel(layer, pt_ref, cache_ref, w1_hbm, pe_ref, w2_ref, kc_ref, vc_ref, tbuf, xbuf, w1buf, xcat, sem,
                     wsem):
    b, c = pl.program_id(0), pl.program_id(1)
    nb, nc = pl.num_programs(0), pl.num_programs(1)
    step = b * nc + c
    slot = step % 2
    n_pages = nc * CMP_PAGES
    mk = functools.partial(_compress_copies, pt_ref, cache_ref, tbuf, sem, layer)

    @pl.when(step == 0)
    def _():
        wcopy = pltpu.make_async_copy(w1_hbm, w1buf, wsem)
        wcopy.start()
        for cp in mk(b, c, slot, n_pages):
            cp.start()
        wcopy.wait()

    @pl.when(step + 1 < nb * nc)
    def _():
        last_c = c == nc - 1
        for cp in mk(jnp.where(last_c, b + 1, b), jnp.where(last_c, 0, c + 1), 1 - slot, n_pages):
            cp.start()

    for cp in mk(b, c, slot, n_pages):
        cp.wait()

    planes_per_out = CMP_PLANES // 2
    for j in range(CMP_PAGES + 1):
        n_rows = PAGE_SIZE if j < CMP_PAGES else CMP_STRIDE
        for t in range(2):
            for q in range(planes_per_out):
                tiles = tbuf[slot, j, t, q * GROUPS_PER_PLANE:(q + 1) * GROUPS_PER_PLANE]
                xbuf[t * planes_per_out + q, j * PAGE_SIZE:j * PAGE_SIZE + n_rows, :] = (
                    tiles.reshape(LANES, PAGE_SIZE).T[:n_rows])

    for t, o_ref in enumerate((kc_ref, vc_ref)):
        for q in range(planes_per_out):
            xs = xbuf.at[t * planes_per_out + q]
            for p in range(CMP_BLOCK):
                x = xs[pl.ds(p, CMP_OUT, stride=CMP_STRIDE), :] + pe_ref[t, p]
                xcat[:, p * LANES:(p + 1) * LANES] = x.astype(BF16)
            h = jnp.dot(xcat[...], w1buf[t], preferred_element_type=F32)
            o_ref[:, q * LANES:(q + 1) * LANES] = jnp.dot(_silu(h).astype(BF16), w2_ref[t],
                                                          preferred_element_type=F32)


def compress_weights(pe, w1, w2):
    eye = jnp.eye(GROUPS_PER_PLANE, dtype=F32)
    w1r = w1.reshape(2, CMP_BLOCK, NSA_HD, CMP_HIDDEN)
    w1bd = jnp.einsum('gh,tpdn->tpgdhn', eye, w1r).reshape(2, CMP_BLOCK * LANES, GROUPS_PER_PLANE * CMP_HIDDEN)
    w2bd = jnp.einsum('gh,tnd->tgnhd', eye, w2).reshape(2, GROUPS_PER_PLANE * CMP_HIDDEN, LANES)
    pet = jnp.tile(pe, (1, 1, GROUPS_PER_PLANE)).reshape(2, CMP_BLOCK, 1, LANES)
    return w1bd.astype(BF16), pet, w2bd.astype(BF16)


def compress_paged(cache, layer, page_table, w1bd, pet, w2bd):
    B, n_pages = page_table.shape
    assert n_pages % CMP_PAGES == 0
    nc = n_pages // CMP_PAGES
    out = jax.ShapeDtypeStruct((B, nc * CMP_OUT, CMP_W), F32)
    ospec = pl.BlockSpec((None, CMP_OUT, CMP_W), lambda b, c, pt: (b, c, 0))
    return pl.pallas_call(
        functools.partial(_compress_kernel, layer),
        grid_spec=pltpu.PrefetchScalarGridSpec(
            num_scalar_prefetch=1,
            grid=(B, nc),
            in_specs=[pl.BlockSpec(memory_space=pl.ANY),
                      pl.BlockSpec(memory_space=pl.ANY),
                      pl.BlockSpec(pet.shape, lambda b, c, pt: (0, 0, 0, 0)),
                      pl.BlockSpec(w2bd.shape, lambda b, c, pt: (0, 0, 0))],
            out_specs=[ospec, ospec],
            scratch_shapes=[pltpu.VMEM((2, CMP_PAGES + 1, 2, NSA_KV, NSA_HD, PAGE_SIZE), F32),
                            pltpu.VMEM((CMP_PLANES, CMP_ROWS + CMP_STRIDE, LANES), F32),
                            pltpu.VMEM(w1bd.shape, BF16),
                            pltpu.VMEM((CMP_OUT, CMP_BLOCK * LANES), BF16),
                            pltpu.SemaphoreType.DMA((2,)),
                            pltpu.SemaphoreType.DMA(())]),
        out_shape=[out, out],
        compiler_params=_cparams("arbitrary", "arbitrary"),
        name="nsa_compress",
    )(page_table, cache, w1bd, pet, w2bd)


NSA_TQ = 256
NSA_TK = 512
SEL_PER_CMP = SEL_BLOCK // CMP_STRIDE
NT_DIMS = (((1,), (1,)), ((), ()))


def _masked_softmax(s, mask):
    s = jnp.where(mask, s, NEG)
    e = jnp.exp(s - jnp.max(s, axis=-1, keepdims=True)) * mask.astype(F32)
    d = jnp.sum(e, axis=-1, keepdims=True)
    return e / jnp.where(d > 0, d, 1.0)


def _pool_matrix(n_cmp, n_sel):
    i = lax.broadcasted_iota(jnp.int32, (n_cmp, n_sel), 0)
    j = lax.broadcasted_iota(jnp.int32, (n_cmp, n_sel), 1)
    lo = SEL_PER_CMP * j - (CMP_BLOCK // CMP_STRIDE - 1)
    return ((i >= lo) & (i <= SEL_PER_CMP * j + SEL_PER_CMP - 1)).astype(BF16)


def _importance(psum, pool):
    return sum(jnp.dot(t, pool, preferred_element_type=F32) for t in _split3(psum))


def _top_k_mask(score, k):
    lane = lax.broadcasted_iota(jnp.int32, score.shape, 1).astype(F32)
    sel = jnp.zeros(score.shape, F32)
    for _ in range(k):
        m = jnp.max(score, axis=-1, keepdims=True)
        idx = jnp.min(jnp.where(score == m, lane, float(score.shape[-1])), axis=-1, keepdims=True)
        hit = lane == idx
        sel = jnp.where(hit, 1.0, sel)
        score = jnp.where(hit, -jnp.inf, score)
    return sel


def _nsa_prompt_kernel(q_ref, qr_ref, kc_ref, vc_ref, ks_ref, vs_ref, kw_ref, vw_ref, gate_ref, o_ref,
                       sel_ref, m_ref, l_ref, acc_ref, oc_ref, ow_ref):
    tq, tk = NSA_TQ, NSA_TK
    s0 = pl.program_id(1) * tq
    tpos = s0 + lax.broadcasted_iota(jnp.int32, (tq, 1), 0)
    n_cmp = kc_ref.shape[0]
    n_sel = sel_ref.shape[1]

    cidx = lax.broadcasted_iota(jnp.int32, (tq, n_cmp), 1)
    cmask = (cidx * CMP_STRIDE + (CMP_BLOCK - 1) <= tpos) & (cidx < n_cmp - 1)
    psum = jnp.zeros((tq, n_cmp), F32)
    for r in range(NSA_REP):
        s = lax.dot_general(q_ref[r], kc_ref[...], NT_DIMS, preferred_element_type=F32)
        p = _masked_softmax(s, cmask)
        oc_ref[r] = jnp.dot(p.astype(BF16), vc_ref[...], preferred_element_type=F32)
        psum = psum + p

    imp = _importance(psum, _pool_matrix(n_cmp, n_sel))
    blk = lax.broadcasted_iota(jnp.int32, (tq, n_sel), 1)
    valid = blk * SEL_BLOCK <= tpos
    forced = (blk == 0) | (blk == tpos // SEL_BLOCK)
    score = jnp.where(forced, jnp.inf, jnp.where(valid, imp, -jnp.inf))
    sel_ref[...] = _top_k_mask(score, N_SEL).astype(BF16)

    band = WINDOW + tq
    w0 = pl.multiple_of(jnp.maximum(s0 - WINDOW, 0), tq)
    wpos = w0 + lax.broadcasted_iota(jnp.int32, (tq, band), 1)
    wmask = (wpos <= tpos) & (wpos > tpos - WINDOW)
    kw = kw_ref[pl.ds(w0, band), :]
    vw = vw_ref[pl.ds(w0, band), :]
    for r in range(NSA_REP):
        s = lax.dot_general(qr_ref[r], kw, NT_DIMS, preferred_element_type=F32)
        ow_ref[r] = jnp.dot(_masked_softmax(s, wmask).astype(BF16), vw, preferred_element_type=F32)

    m_ref[...] = jnp.full(m_ref.shape, NEG, F32)
    l_ref[...] = jnp.zeros(l_ref.shape, F32)
    acc_ref[...] = jnp.zeros(acc_ref.shape, F32)

    def sweep(kt, carry):
        k0 = pl.multiple_of(kt * tk, tk)
        jj = lax.broadcasted_iota(jnp.int32, (n_sel, tk), 0)
        ss = lax.broadcasted_iota(jnp.int32, (n_sel, tk), 1)
        expand = (jj == kt * (tk // SEL_BLOCK) + ss // SEL_BLOCK).astype(BF16)
        chosen = jnp.dot(sel_ref[...], expand, preferred_element_type=F32)
        kpos = k0 + lax.broadcasted_iota(jnp.int32, (tq, tk), 1)
        allowed = (chosen > 0.5) & (kpos <= tpos)
        ks = ks_ref[pl.ds(k0, tk), :]
        vs = vs_ref[pl.ds(k0, tk), :]
        for r in range(NSA_REP):
            s = lax.dot_general(qr_ref[r], ks, NT_DIMS, preferred_element_type=F32)
            s = jnp.where(allowed, s, NEG)
            m_old = m_ref[r]
            m_new = jnp.maximum(m_old, jnp.max(s, axis=-1, keepdims=True))
            p = jnp.exp(s - m_new)
            alpha = jnp.exp(m_old - m_new)
            l_ref[r] = alpha * l_ref[r] + jnp.sum(p, axis=-1, keepdims=True)
            acc_ref[r] = alpha * acc_ref[r] + jnp.dot(p.astype(BF16), vs, preferred_element_type=F32)
            m_ref[r] = m_new
        return carry

    lax.fori_loop(0, (s0 + tq + tk - 1) // tk, sweep, 0)

    gates = _sigmoid(gate_ref[...])
    for r in range(NSA_REP):
        o_s = acc_ref[r] / l_ref[r]
        o = (gates[:, 3 * r:3 * r + 1] * oc_ref[r] + gates[:, 3 * r + 1:3 * r + 2] * o_s
             + gates[:, 3 * r + 2:3 * r + 3] * ow_ref[r])
        o_ref[:, r * NSA_HD:(r + 1) * NSA_HD] = o.astype(o_ref.dtype)


def nsa_prompt(q, qr, kc, vc, ks, vs, kw, vw, gates):
    L = q.shape[1]
    n_cmp = kc.shape[1]
    n_sel = L // SEL_BLOCK
    assert L % NSA_TK == 0 and L >= WINDOW + NSA_TQ
    qspec = pl.BlockSpec((NSA_REP, NSA_TQ, NSA_HD), lambda g, i: (g, i, 0))
    cspec = pl.BlockSpec((None, n_cmp, NSA_HD), lambda g, i: (g, 0, 0))
    kspec = pl.BlockSpec((None, L, NSA_HD), lambda g, i: (g, 0, 0))
    per_head = lambda w: pltpu.VMEM((NSA_REP, NSA_TQ, w), F32)
    return pl.pallas_call(
        _nsa_prompt_kernel,
        grid=(NSA_KV, L // NSA_TQ),
        in_specs=[qspec, qspec, cspec, cspec, kspec, kspec, kspec, kspec,
                  pl.BlockSpec((None, NSA_TQ, LANES), lambda g, i: (g, i, 0))],
        out_specs=pl.BlockSpec((NSA_TQ, NSA_REP * NSA_HD), lambda g, i: (i, g)),
        out_shape=jax.ShapeDtypeStruct((L, NSA_HEADS * NSA_HD), BF16),
        scratch_shapes=[pltpu.VMEM((NSA_TQ, n_sel), BF16),
                        per_head(1), per_head(1),
                        per_head(NSA_HD), per_head(NSA_HD), per_head(NSA_HD)],
        compiler_params=_cparams("parallel", "arbitrary"),
        name="nsa_prompt",
    )(q, qr, kc, vc, ks, vs, kw, vw, gates)


def _top_k_indices(score, k):
    lane = lax.broadcasted_iota(jnp.int32, score.shape, 1).astype(F32)
    out = jnp.zeros(score.shape, F32)
    for t in range(k):
        m = jnp.max(score, axis=-1, keepdims=True)
        idx = jnp.min(jnp.where(score == m, lane, float(score.shape[-1])), axis=-1, keepdims=True)
        out = jnp.where(lane == float(t), idx, out)
        score = jnp.where(lane == idx, -jnp.inf, score)
    return out.astype(jnp.int32)


def _nsa_sample_select_kernel(q_pos, q_ref, kc_ref, vc_ref, oc_ref, idx_ref):
    n_cmp = kc_ref.shape[0]
    n_sel = idx_ref.shape[-1]
    q = q_ref[...]
    head_group = lax.broadcasted_iota(jnp.int32, (NSA_HEADS, 1), 0) // NSA_REP
    cidx = lax.broadcasted_iota(jnp.int32, (NSA_HEADS, n_cmp), 1)
    cmask = (cidx * CMP_STRIDE + (CMP_BLOCK - 1) <= q_pos) & (cidx < n_cmp - 1)
    grp_row = lax.broadcasted_iota(jnp.int32, (NSA_KV, 1), 0)
    oc = jnp.zeros((NSA_HEADS, NSA_HD), F32)
    psum = jnp.zeros((NSA_KV, n_cmp), F32)
    for g in range(NSA_KV):
        cols = slice(g * NSA_HD, (g + 1) * NSA_HD)
        s = lax.dot_general(q, kc_ref[:, cols].astype(BF16), NT_DIMS, preferred_element_type=F32)
        p = _masked_softmax(s, cmask)
        o = jnp.dot(p.astype(BF16), vc_ref[:, cols].astype(BF16), preferred_element_type=F32)
        mine = head_group == g
        oc = jnp.where(mine, o, oc)
        pg = jnp.sum(jnp.where(mine, p, 0.0), axis=0, keepdims=True)
        psum = jnp.where(grp_row == g, pg, psum)
    oc_ref[...] = oc
    imp = _importance(psum, _pool_matrix(n_cmp, n_sel))
    blk = lax.broadcasted_iota(jnp.int32, (NSA_KV, n_sel), 1)
    score = jnp.where(blk == 0, jnp.inf, imp)
    idx_ref[...] = _top_k_indices(score, N_SEL - 1)


def nsa_sample_select(q, kc, vc, q_pos, n_sel):
    B, n_cmp = kc.shape[0], kc.shape[1]
    cspec = pl.BlockSpec((None, n_cmp, CMP_W), lambda b: (b, 0, 0))
    hspec = pl.BlockSpec((None, NSA_HEADS, NSA_HD), lambda b: (b, 0, 0))
    return pl.pallas_call(
        functools.partial(_nsa_sample_select_kernel, q_pos),
        grid=(B,),
        in_specs=[hspec, cspec, cspec],
        out_specs=[hspec, pl.BlockSpec((None, NSA_KV, n_sel), lambda b: (b, 0, 0))],
        out_shape=[jax.ShapeDtypeStruct((B, NSA_HEADS, NSA_HD), F32),
                   jax.ShapeDtypeStruct((B, NSA_KV, n_sel), jnp.int32)],
        compiler_params=_cparams("parallel"),
        name="nsa_sample_select",
    )(q, kc, vc)


N_PAST_SEL = N_SEL - 1
SEL_K, SEL_V = 2, 3
BLOCKS_PER_PAGE = PAGE_SIZE // SEL_BLOCK


def _sel_copies(pt_ref, idx_ref, cache_ref, kbuf, vbuf, sem, layer, n_pages, b, slot):
    out = []
    for g in range(NSA_KV):
        for t in range(N_PAST_SEL):
            blk = idx_ref[(b * NSA_KV + g) * N_SEL + t]
            pg = pt_ref[b * n_pages + blk // BLOCKS_PER_PAGE]
            dst = pl.ds(t * PAGE_SIZE, PAGE_SIZE)
            out.append(pltpu.make_async_copy(cache_ref.at[layer, pg, SEL_K, g], kbuf.at[slot, g, :, dst], sem.at[slot]))
            out.append(pltpu.make_async_copy(cache_ref.at[layer, pg, SEL_V, g], vbuf.at[slot, g, :, dst], sem.at[slot]))
    return out


def _softmax_with_self(s, s_self, mask=None):
    if mask is not None:
        s = jnp.where(mask, s, NEG)
    m = jnp.maximum(jnp.max(s, axis=-1, keepdims=True), s_self)
    e = jnp.exp(s - m)
    if mask is not None:
        e = e * mask.astype(F32)
    e_self = jnp.exp(s_self - m)
    return e, e_self, jnp.sum(e, axis=-1, keepdims=True) + e_self


def _nsa_sample_attend_kernel(layer, n_pages, win_first, pt_ref, idx_ref, cache_ref, qr_ref, new_ref, win_ref, oc_ref,
                              gate_ref, o_ref, kbuf, vbuf, sem):
    b = pl.program_id(0)
    nb = pl.num_programs(0)
    slot = b % 2
    mk = functools.partial(_sel_copies, pt_ref, idx_ref, cache_ref, kbuf, vbuf, sem, layer, n_pages)

    @pl.when(b == 0)
    def _():
        for cp in mk(b, slot):
            cp.start()

    @pl.when(b + 1 < nb)
    def _():
        for cp in mk(b + 1, 1 - slot):
            cp.start()

    for cp in mk(b, slot):
        cp.wait()

    qr = qr_ref[...]
    qf = qr.astype(F32)
    head_group = lax.broadcasted_iota(jnp.int32, (NSA_HEADS, 1), 0) // NSA_REP
    gates = _sigmoid(gate_ref[...])
    n_win = win_ref.shape[-1]
    n_key = N_PAST_SEL * PAGE_SIZE
    wmask = lax.broadcasted_iota(jnp.int32, (NSA_HEADS, n_win), 1) >= win_first
    key = lax.broadcasted_iota(jnp.int32, (NSA_HEADS, n_key), 1)
    key_block = (key % PAGE_SIZE) // SEL_BLOCK
    out = jnp.zeros((NSA_HEADS, NSA_HD), F32)
    for g in range(NSA_KV):
        cols = slice(g * NSA_HD, (g + 1) * NSA_HD)
        ks_new, vs_new = new_ref[0:1, cols], new_ref[1:2, cols]
        kw_new, vw_new = new_ref[2:3, cols], new_ref[3:4, cols]
        want = jnp.zeros((NSA_HEADS, n_key), jnp.int32)
        for t in range(N_PAST_SEL):
            blk = idx_ref[(b * NSA_KV + g) * N_SEL + t]
            want = jnp.where(key // PAGE_SIZE == t, blk % BLOCKS_PER_PAGE, want)
        s = jnp.dot(qr, kbuf[slot, g].astype(BF16), preferred_element_type=F32)
        e, e_self, d = _softmax_with_self(s, jnp.sum(qf * ks_new, axis=-1, keepdims=True), key_block == want)
        o_s = (lax.dot_general(e.astype(BF16), vbuf[slot, g].astype(BF16), NT_DIMS, preferred_element_type=F32)
               + e_self * vs_new) / d
        s = jnp.dot(qr, win_ref[0, g].astype(BF16), preferred_element_type=F32)
        e, e_self, d = _softmax_with_self(s, jnp.sum(qf * kw_new, axis=-1, keepdims=True), wmask)
        o_w = (lax.dot_general(e.astype(BF16), win_ref[1, g].astype(BF16), NT_DIMS, preferred_element_type=F32)
               + e_self * vw_new) / d
        o = gates[:, 0:1] * oc_ref[...] + gates[:, 1:2] * o_s + gates[:, 2:3] * o_w
        out = jnp.where(head_group == g, o, out)
    o_ref[...] = out


def nsa_sample_attend(cache, layer, page_table, idx, qr, new_rows, win, o_cmp, gates, win_first):
    B, n_pages = page_table.shape
    n_win = win.shape[-1]
    hspec = lambda: pl.BlockSpec((None, NSA_HEADS, NSA_HD), lambda b, pt, ix: (b, 0, 0))
    return pl.pallas_call(
        functools.partial(_nsa_sample_attend_kernel, layer, n_pages, win_first),
        grid_spec=pltpu.PrefetchScalarGridSpec(
            num_scalar_prefetch=2,
            grid=(B,),
            in_specs=[pl.BlockSpec(memory_space=pl.ANY),
                      hspec(),
                      pl.BlockSpec((None, 4, CMP_W), lambda b, pt, ix: (b, 0, 0)),
                      pl.BlockSpec((None, None, 2, NSA_KV, NSA_HD, n_win), lambda b, pt, ix: (layer, b, 0, 0, 0, 0)),
                      hspec(),
                      pl.BlockSpec((None, NSA_HEADS, LANES), lambda b, pt, ix: (b, 0, 0))],
            out_specs=hspec(),
            scratch_shapes=[pltpu.VMEM((2, NSA_KV, NSA_HD, N_PAST_SEL * PAGE_SIZE), F32),
                            pltpu.VMEM((2, NSA_KV, NSA_HD, N_PAST_SEL * PAGE_SIZE), F32),
                            pltpu.SemaphoreType.DMA((2,))]),
        out_shape=jax.ShapeDtypeStruct((B, NSA_HEADS, NSA_HD), F32),
        compiler_params=_cparams("arbitrary"),
        name="nsa_sample_attend",
    )(page_table.reshape(-1), idx.reshape(-1), cache, qr, new_rows, win, o_cmp, gates)


def _merge_kernel(oa_ref, ob_ref, wa_ref, wb_ref, ma_ref, mb_ref, o_ref, wa_bf, wb_bf):
    @pl.when(pl.program_id(1) == 0)
    def _():
        wa_bf[...] = wa_ref[...].astype(BF16)
        wb_bf[...] = wb_ref[...].astype(BF16)

    ya = jnp.dot(oa_ref[...], wa_bf[...], preferred_element_type=F32)
    yb = jnp.dot(ob_ref[...], wb_bf[...], preferred_element_type=F32)
    o_ref[...] = (_sigmoid(ma_ref[...]) * ya + _sigmoid(mb_ref[...]) * yb).astype(o_ref.dtype)


def merge(o_a, o_b, w_a, w_b, zm, layer, *, tm, tn=512):
    M, K = o_a.shape
    D = w_a.shape[-1]
    xspec = pl.BlockSpec((tm, K), lambda j, i: (i, 0))
    wspec = pl.BlockSpec((None, K, tn), lambda j, i: (layer, 0, j))
    return pl.pallas_call(
        _merge_kernel,
        grid=(D // tn, M // tm),
        in_specs=[xspec, xspec, wspec, wspec,
                  pl.BlockSpec((tm, tn), lambda j, i: (i, j)),
                  pl.BlockSpec((tm, tn), lambda j, i: (i, j + D // tn))],
        out_specs=pl.BlockSpec((tm, tn), lambda j, i: (i, j)),
        out_shape=jax.ShapeDtypeStruct((M, D), BF16),
        scratch_shapes=[pltpu.VMEM((K, tn), BF16), pltpu.VMEM((K, tn), BF16)],
        compiler_params=_cparams("arbitrary", "arbitrary"),
        name="merge",
    )(o_a, o_b, w_a, w_b, zm, zm)


MOE_TILE = 256
ROUTE_TILE = 256


def _first_argmax(vals):
    best, idx = vals[0], jnp.zeros(vals[0].shape, F32)
    for e in range(1, len(vals)):
        better = vals[e] > best
        idx = jnp.where(better, float(e), idx)
        best = jnp.where(better, vals[e], best)
    return best, idx


def _router_kernel(x_ref, w_ref, b_ref, o_ref):
    xs = _split3(x_ref[...])
    ws = _split3(w_ref[...])
    terms = ((0, 0), (0, 1), (1, 0), (0, 2), (2, 0), (1, 1))
    logits = b_ref[...] + sum(lax.dot_general(ws[a], xs[c], NT_DIMS, preferred_element_type=F32) for a, c in terms)
    m = jnp.max(logits, axis=0, keepdims=True)
    e = jnp.exp(logits - m)
    probs = e / jnp.sum(e, axis=0, keepdims=True)
    p = [probs[i:i + 1] for i in range(N_EXPERTS)]
    gscore = []
    for g in range(N_GROUPS):
        mem = p[g * EXPERTS_PER_GROUP:(g + 1) * EXPERTS_PER_GROUP]
        pairs = [mem[a] + mem[c] for a in range(EXPERTS_PER_GROUP) for c in range(a + 1, EXPERTS_PER_GROUP)]
        gscore.append(functools.reduce(jnp.maximum, pairs))
    _, g_sel = _first_argmax(gscore)
    masked = [jnp.where(g_sel == float(i // EXPERTS_PER_GROUP), p[i], -1.0) for i in range(N_EXPERTS)]
    v1, i1 = _first_argmax(masked)
    masked2 = [jnp.where(i1 == float(i), -2.0, masked[i]) for i in range(N_EXPERTS)]
    v2, i2 = _first_argmax(masked2)
    tot = v1 + v2
    o_ref[...] = jnp.concatenate([i1, i2, v1 / tot, v2 / tot, jnp.zeros((4,) + i1.shape[1:], F32)], axis=0)


def moe_route(x, w_router, b_router):
    T, D = x.shape
    return pl.pallas_call(
        _router_kernel,
        grid=(T // ROUTE_TILE,),
        in_specs=[pl.BlockSpec((ROUTE_TILE, D), lambda i: (i, 0)),
                  pl.BlockSpec((N_EXPERTS, D), lambda i: (0, 0)),
                  pl.BlockSpec((N_EXPERTS, 1), lambda i: (0, 0))],
        out_specs=pl.BlockSpec((8, ROUTE_TILE), lambda i: (0, i)),
        out_shape=jax.ShapeDtypeStruct((8, T), F32),
        compiler_params=_cparams("parallel"),
        name="moe_route",
    )(x, w_router.T, b_router.reshape(N_EXPERTS, 1))


def _start_row_copies(n, hbm, rows_ref, base, buf, sub, sem, to_hbm):
    def body(j, carry):
        win = buf.at[pl.ds(pl.multiple_of(j * sub, sub), sub), :]
        row = hbm.at[pl.ds(pl.multiple_of(rows_ref[base + j] * sub, sub), sub), :]
        (pltpu.make_async_copy(win, row, sem) if to_hbm else pltpu.make_async_copy(row, win, sem)).start()
        return carry

    lax.fori_loop(0, n, body, 0)


def _wait_row_copies(n, hbm, buf, sub, sem, to_hbm):
    size = MOE_TILE
    while size:
        def wait(size=size):
            a, b = buf.at[pl.ds(0, size * sub), :], hbm.at[pl.ds(0, size * sub), :]
            (pltpu.make_async_copy(a, b, sem) if to_hbm else pltpu.make_async_copy(b, a, sem)).wait()

        if isinstance(n, int):
            if n & size:
                wait()
        else:
            pl.when((n & size) != 0)(wait)
        size //= 2


def _expert_ffn_kernel(src_ref, dst_ref, te_ref, nv_ref, nt_ref, xw_hbm, w1_ref, w3_ref, w2_ref, y_hbm,
                       xbuf, x2d, y2d, obuf, gsem, ssem):
    t = pl.program_id(0)
    nt = nt_ref[0]
    slot = t % 2
    xsub = xbuf.shape[1] // MOE_TILE
    ysub = obuf.shape[0] // MOE_TILE
    start_gather = lambda tile, sl: _start_row_copies(MOE_TILE, xw_hbm, src_ref, tile * MOE_TILE, xbuf.at[sl], xsub,
                                                      gsem.at[sl], False)
    wait_scatter = lambda tile: _wait_row_copies(nv_ref[tile], y_hbm, obuf, ysub, ssem, True)

    @pl.when((t == 0) & (nt > 0))
    def _():
        start_gather(t, slot)

    @pl.when(t + 1 < nt)
    def _():
        start_gather(t + 1, 1 - slot)

    @pl.when(t < nt)
    def _():
        _wait_row_copies(MOE_TILE, xw_hbm, xbuf.at[slot], xsub, gsem.at[slot], False)
        xs = xbuf.at[slot]
        for r in range(MOE_TILE // SUBLANES):
            for c in range(xsub):
                x2d[r * SUBLANES:(r + 1) * SUBLANES, c * LANES:(c + 1) * LANES] = (
                    xs[pl.ds(r * SUBLANES * xsub + c, SUBLANES, stride=xsub), :])
        x = x2d[...].astype(BF16)
        h = _silu(jnp.dot(x, w1_ref[...], preferred_element_type=F32)) * jnp.dot(x, w3_ref[...],
                                                                              preferred_element_type=F32)
        y2d[...] = jnp.dot(h.astype(BF16), w2_ref[...], preferred_element_type=F32)

        @pl.when(t > 0)
        def _():
            wait_scatter(t - 1)

        for r in range(MOE_TILE // SUBLANES):
            for c in range(ysub):
                obuf[pl.ds(r * SUBLANES * ysub + c, SUBLANES, stride=ysub), :] = (
                    y2d[r * SUBLANES:(r + 1) * SUBLANES, c * LANES:(c + 1) * LANES])
        _start_row_copies(nv_ref[t], y_hbm, dst_ref, t * MOE_TILE, obuf, ysub, ssem, True)

        @pl.when(t == nt - 1)
        def _():
            wait_scatter(t)


def expert_ffn(xw, src_token, dst_row, n_valid, n_rows_out, tile_expert, n_tiles_used, w1, w3, w2, layer):
    P = src_token.shape[0]
    D, F = w1.shape[2], w1.shape[3]
    xsub = ysub = D // LANES
    wspec = lambda a, c: pl.BlockSpec((None, None, a, c), lambda t, src, dst, te, nv, nt: (layer, te[t], 0, 0))
    return pl.pallas_call(
        _expert_ffn_kernel,
        grid_spec=pltpu.PrefetchScalarGridSpec(
            num_scalar_prefetch=5,
            grid=(P // MOE_TILE,),
            in_specs=[pl.BlockSpec(memory_space=pl.ANY), wspec(D, F), wspec(D, F), wspec(F, D)],
            out_specs=pl.BlockSpec(memory_space=pl.ANY),
            scratch_shapes=[pltpu.VMEM((2, MOE_TILE * xsub, LANES), F32),
                            pltpu.VMEM((MOE_TILE, xsub * LANES), F32),
                            pltpu.VMEM((MOE_TILE, D), F32),
                            pltpu.VMEM((MOE_TILE * ysub, LANES), F32),
                            pltpu.SemaphoreType.DMA((2,)),
                            pltpu.SemaphoreType.DMA(())]),
        out_shape=jax.ShapeDtypeStruct((n_rows_out * ysub, LANES), F32),
        compiler_params=_cparams("arbitrary"),
        name="expert_ffn",
    )(src_token, dst_row, tile_expert, n_valid, n_tiles_used, xw, w1, w3, w2)


def _combine_ln_kernel(alpha, x_ref, y1_ref, y2_ref, wt_ref, g_ref, b_ref, o_ref, ob_ref, y_scr):
    tm, D = o_ref.shape
    sub = D // LANES
    for r in range(tm // SUBLANES):
        rows = slice(r * SUBLANES, (r + 1) * SUBLANES)
        w1, w2 = wt_ref[rows, 0:1], wt_ref[rows, 1:2]
        for c in range(sub):
            pick = pl.ds(r * SUBLANES * sub + c, SUBLANES, stride=sub)
            y_scr[rows, c * LANES:(c + 1) * LANES] = w1 * y1_ref[pick, :] + w2 * y2_ref[pick, :]
    out = _layer_norm(alpha * x_ref[...] + y_scr[...], g_ref[...], b_ref[...])
    o_ref[...] = out
    ob_ref[...] = out.astype(BF16)


def moe_combine_layer_norm(x, ysel, wts, g, b, alpha, *, tm=256):
    T, D = x.shape
    nt = T // tm
    sub = D // LANES
    spec = pl.BlockSpec((tm, D), lambda i: (i, 0))
    vec = pl.BlockSpec((1, D), lambda i: (0, 0))
    return pl.pallas_call(
        functools.partial(_combine_ln_kernel, alpha),
        grid=(nt,),
        in_specs=[spec, pl.BlockSpec((tm * sub, LANES), lambda i: (i, 0)),
                  pl.BlockSpec((tm * sub, LANES), lambda i: (i + nt, 0)),
                  pl.BlockSpec((tm, 2), lambda i: (i, 0)), vec, vec],
        out_specs=[spec, spec],
        out_shape=[jax.ShapeDtypeStruct((T, D), F32), jax.ShapeDtypeStruct((T, D), BF16)],
        scratch_shapes=[pltpu.VMEM((tm, D), F32)],
        compiler_params=_cparams("parallel"),
        name="moe_combine_ln",
    )(x, ysel, ysel, wts, g.reshape(1, D), b.reshape(1, D))


def moe_dispatch_plan(e1, e2, n_tiles):
    T = e1.shape[0]
    ea = jnp.concatenate([e1, e2])
    onehot = (ea[:, None] == jnp.arange(N_EXPERTS)[None, :]).astype(jnp.int32)
    rank = jnp.take_along_axis(jnp.cumsum(onehot, axis=0) - onehot, ea[:, None], axis=1)[:, 0]
    counts = onehot.sum(0)
    tiles = (counts + MOE_TILE - 1) // MOE_TILE
    tile_end = jnp.cumsum(tiles)
    dest = ((tile_end - tiles) * MOE_TILE)[ea] + rank
    n_slots = n_tiles * MOE_TILE
    src_token = jnp.zeros((n_slots,), jnp.int32).at[dest].set(jnp.tile(jnp.arange(T, dtype=jnp.int32), 2))
    dst_row = jnp.zeros((n_slots,), jnp.int32).at[dest].set(jnp.arange(2 * T, dtype=jnp.int32))
    tile_idx = jnp.arange(n_tiles)
    tile_expert = jnp.minimum((tile_end[None, :] <= tile_idx[:, None]).sum(-1), N_EXPERTS - 1)
    first_tile = (tile_end - tiles)[tile_expert]
    n_valid = jnp.clip(counts[tile_expert] - (tile_idx - first_tile) * MOE_TILE, 0, MOE_TILE)
    return (src_token, dst_row, tile_expert.astype(jnp.int32), n_valid.astype(jnp.int32),
            tile_end[-1:].astype(jnp.int32))


def moe_block(x, x_slabs, w_router, b_router, w1, w3, w2, layer, g, b, alpha):
    T, D = x.shape
    r = moe_route(x, w_router, b_router)
    e1, e2 = r[0].astype(jnp.int32), r[1].astype(jnp.int32)
    n_tiles = 2 * T // MOE_TILE + N_EXPERTS
    src_token, dst_row, tile_expert, n_valid, n_used = moe_dispatch_plan(e1, e2, n_tiles)
    ysel = expert_ffn(x_slabs, src_token, dst_row, n_valid, 2 * T, tile_expert, n_used, w1, w3, w2, layer)
    return moe_combine_layer_norm(x, ysel, r[2:4].T, g, b, alpha)


def _hgrn_sample_kernel(qt_ref, ft_ref, lbt_ref, onemt_ref, ia_ref, ga_ref, gn_ref, s0_ref, o_ref, s_ref):
    qa = qt_ref[...]
    fa = ft_ref[...]
    q_all = _silu(qa)
    sig = _sigmoid(fa)
    f_all = lbt_ref[...] + onemt_ref[...] * sig
    k_all = onemt_ref[...] * _sigmoid(-fa)
    for h in range(A_HEADS):
        cols = slice(h * A_DK, (h + 1) * A_DK)
        v = ia_ref[:, cols]
        s_new = f_all[:, h:h + 1] * s0_ref[h] + k_all[:, h:h + 1] * v
        s_ref[h] = s_new
        o = jnp.sum(s_new * q_all[:, h:h + 1], axis=0, keepdims=True)
        o_ref[:, cols] = _hgrn_gate_norm(o, ga_ref[:, cols], gn_ref[:, cols]).astype(o_ref.dtype)


def hgrn_sample(qa_t, fa_t, lb_t, onem_t, ia, ga, gnorm, state, layer):
    B = qa_t.shape[0]
    W = A_HEADS * A_DK
    col = pl.BlockSpec((None, A_DK, A_HEADS), lambda b: (b, 0, 0))
    par = pl.BlockSpec((A_DK, A_HEADS), lambda b: (0, 0))
    row = pl.BlockSpec((None, 1, W), lambda b: (b, 0, 0))
    return pl.pallas_call(
        _hgrn_sample_kernel,
        grid=(B,),
        in_specs=[col, col, par, par, row, row, pl.BlockSpec((1, W), lambda b: (0, 0)),
                  pl.BlockSpec((None, None, A_HEADS, A_DK, A_DK), lambda b: (layer, b, 0, 0, 0))],
        out_specs=[row, pl.BlockSpec((None, A_HEADS, A_DK, A_DK), lambda b: (b, 0, 0, 0))],
        out_shape=[jax.ShapeDtypeStruct((B, 1, W), BF16), jax.ShapeDtypeStruct((B, A_HEADS, A_DK, A_DK), F32)],
        compiler_params=_cparams("parallel"),
        name="hgrn_sample",
    )(qa_t, fa_t, lb_t, onem_t, ia, ga, gnorm, state)


MM_TILE_M = 768
MM_TILE_N = 512
W_A = A_HEADS * A_DK
W_Q = NSA_HEADS * NSA_HD
COL_QB = 4 * W_A
COL_KV = COL_QB + W_Q
COL_GB = COL_KV + 6 * CMP_W
COL_MA = COL_GB + 3 * NSA_HEADS


def _pad_lanes(a):
    return jnp.pad(a, [(0, 0)] * (a.ndim - 1) + [(0, LANES - a.shape[-1])])


def token_minor(a):
    return jnp.transpose(a, (0, 1, 3, 4, 5, 2))

def kernel(x_prompt, x_sample, cache_nsa_kv, cache_nsa_win, state_hgrn, page_table, w_in, lb_raw, gnorm_a, cmp_pe, cmp_w1, cmp_w2, w_branch_a, w_branch_b, w_out, ln1_g, ln1_b, ln2_g, ln2_b, w_router, b_router, w_e1, w_e3, w_e2):
    depth, D = w_in.shape[0], w_in.shape[1]
    L, B = x_prompt.shape[1], x_sample.shape[0]
    assert x_prompt.shape[0] == 1 and x_sample.shape[1] == 1
    n_pool = cache_nsa_kv.shape[1]
    past_len = page_table.shape[1] * PAGE_SIZE
    win_rows = cache_nsa_win.shape[2]
    T = L + B
    M = -(-T // MM_TILE_M) * MM_TILE_M
    alpha = (2 * depth) ** 0.25
    prompt, sample = slice(0, L), slice(L, T)

    lbs = jnp.cumsum(jax.nn.softmax(lb_raw.astype(F32), axis=0), axis=0)
    lbs = lbs - lbs[0:1]
    pos = jnp.concatenate([jnp.arange(L), jnp.full((B,), past_len), jnp.zeros((M - T,), jnp.int32)])
    cos, sin = rope_tables(pos)
    ones, zeros = jnp.ones_like(cos), jnp.zeros_like(cos)
    cache = token_minor(cache_nsa_kv)
    win_cache = token_minor(cache_nsa_win)
    w1b, w3b, w2b = w_e1.astype(BF16), w_e3.astype(BF16), w_e2.astype(BF16)
    scale = NSA_HD ** -0.5

    x = jnp.concatenate([x_prompt[0], x_sample[:, 0], jnp.zeros((M - T, D), F32)])
    xb = x.astype(BF16)
    heads_major = lambda a, h: a.reshape(a.shape[0], h, NSA_HD).transpose(1, 0, 2)
    col_major = lambda a: a.reshape(B, A_HEADS, A_DK).transpose(0, 2, 1)
    outs = [[] for _ in range(6)]
    for l in range(depth):
        lb = lbs[l]
        mm = functools.partial(matmul, tm=MM_TILE_M, tn=MM_TILE_N)
        z = mm(xb, w_in, l, col0=0, ncols=COL_GB)
        zg = matmul(xb, _pad_lanes(w_in[l, :, COL_GB:COL_MA])[None], 0, tm=MM_TILE_M, tn=LANES)
        zm = mm(xb, w_in[l, :, COL_MA:][None], 0)

        o_a_p, st_p = hgrn_prompt(z, L, jnp.log(lb)[None], jnp.log1p(-lb)[None], (1.0 - lb)[None], gnorm_a[l][None])
        zs = z[sample]
        o_a_s, st_s = hgrn_sample(col_major(zs[:, :W_A]), col_major(zs[:, W_A:2 * W_A]),
                                  lb.reshape(A_HEADS, A_DK).T, (1.0 - lb).reshape(A_HEADS, A_DK).T,
                                  zs[:, None, 2 * W_A:3 * W_A], zs[:, None, 3 * W_A:4 * W_A], gnorm_a[l][None],
                                  state_hgrn, l)
        o_a = jnp.concatenate([o_a_p, o_a_s[:, 0], jnp.zeros((M - T, W_A), BF16)])

        q_plain = rope(z, COL_QB, W_Q, ones, zeros, scale=scale, out_dtype=BF16)
        q_rot = rope(z, COL_QB, W_Q, cos, sin, scale=scale, out_dtype=BF16)
        ks_rot = rope(z, COL_KV + 2 * CMP_W, CMP_W, cos, sin)
        kw_rot = rope(z, COL_KV + 4 * CMP_W, CMP_W, cos, sin)
        v_sel = z[:, COL_KV + 3 * CMP_W:COL_KV + 4 * CMP_W]
        v_win = z[:, COL_KV + 5 * CMP_W:COL_KV + 6 * CMP_W]
        rows = jnp.concatenate([z[:, COL_KV:COL_KV + 2 * CMP_W], ks_rot, v_sel], axis=-1)
        win_new = jnp.concatenate([kw_rot, v_win], axis=-1)
        cw = compress_weights(cmp_pe[l], cmp_w1[l], cmp_w2[l])

        prompt_pages = token_minor(rows[prompt].reshape(1, L // PAGE_SIZE, PAGE_SIZE, 4, NSA_KV, NSA_HD))
        kc_p, vc_p = compress_paged(prompt_pages, 0, jnp.arange(L // PAGE_SIZE, dtype=jnp.int32)[None], *cw)
        grp = lambda a: heads_major(a, NSA_KV).astype(BF16)
        gates_p =_pad_lanes(zg[prompt, :3 * NSA_HEADS].reshape(L, NSA_KV, 3 * NSA_REP).transpose(1, 0, 2))
        o_b_p = nsa_prompt(heads_major(q_plain[prompt], NSA_HEADS), heads_major(q_rot[prompt], NSA_HEADS),
                           grp(kc_p[0]), grp(vc_p[0]), grp(ks_rot[prompt]), grp(v_sel[prompt]),
                           grp(kw_rot[prompt]), grp(v_win[prompt]), gates_p)

        kc_s, vc_s = compress_paged(cache, l, page_table, *cw)
        o_c_s, idx = nsa_sample_select(q_plain[sample].reshape(B, NSA_HEADS, NSA_HD), kc_s, vc_s,
                                       past_len, past_len // SEL_BLOCK)
        rs, ws = rows[sample], win_new[sample]
        new_rows = jnp.stack([rs[:, 2 * CMP_W:3 * CMP_W], rs[:, 3 * CMP_W:], ws[:, :CMP_W], ws[:, CMP_W:]], axis=1)
        gates_s = _pad_lanes(zg[sample, :3 * NSA_HEADS].reshape(B, NSA_HEADS, 3))
        o_b_s = nsa_sample_attend(cache, l, page_table, idx[:, :, :N_SEL], q_rot[sample].reshape(B, NSA_HEADS, NSA_HD),
                                  new_rows, win_cache, o_c_s, gates_s, win_rows - WINDOW + 1)
        o_b = jnp.concatenate([o_b_p, o_b_s.reshape(B, W_Q).astype(BF16), jnp.zeros((M - T, W_Q), BF16)])

        h = merge(o_a, o_b, w_branch_a, w_branch_b, zm, l, tm=MM_TILE_M)
        y = mm(h, w_out, l)
        x1, x1_slabs = residual_layer_norm(x, y, ln1_g[l], ln1_b[l], alpha)
        x, xb = moe_block(x1, x1_slabs, w_router, b_router, w1b, w3b, w2b, l, ln2_g[l], ln2_b[l], alpha)

        wbp = min(WINDOW, L)
        outs[0].append(rows[prompt].reshape(1, L, 4, NSA_KV, NSA_HD))
        outs[1].append(rs.reshape(B, 1, 4, NSA_KV, NSA_HD))
        outs[2].append(win_new[L - wbp:L].reshape(1, wbp, 2, NSA_KV, NSA_HD))
        outs[3].append(jnp.concatenate([cache_nsa_win[l][:, 1:], ws.reshape(B, 1, 2, NSA_KV, NSA_HD)], axis=1))
        outs[4].append(st_p[None])
        outs[5].append(st_s)
    return (x[prompt][None], x[sample][:, None]) + tuple(jnp.stack(o) for o in outs)
```

```python
import functools
import math

import jax
import jax.numpy as jnp
import numpy as np
from jax import lax
from jax.experimental import pallas as pl
from jax.experimental.pallas import tpu as pltpu

F32 = jnp.float32
BF16 = jnp.bfloat16

A_HEADS = 8
A_DK = 128
NSA_HEADS = 16
NSA_KV = 4
NSA_REP = NSA_HEADS // NSA_KV
NSA_HD = 64
CMP_BLOCK = 32
CMP_STRIDE = 16
CMP_HIDDEN = 128
SEL_BLOCK = 64
N_SEL = 16
WINDOW = 512
PAGE_SIZE = 128
ROPE_THETA = 10000.0
N_EXPERTS = 16
N_GROUPS = 4
EXPERTS_PER_GROUP = N_EXPERTS // N_GROUPS
LN_EPS = 1e-5
NEG = -1e30

VMEM_LIMIT_BYTES = 56 * 1024 * 1024
LANES = 128


def _cparams(*sem):
    return pltpu.CompilerParams(dimension_semantics=sem, vmem_limit_bytes=VMEM_LIMIT_BYTES)


def _sigmoid(x):
    return 1.0 / (1.0 + jnp.exp(-x))


def _silu(x):
    return x * _sigmoid(x)


def _mm_kernel(x_ref, w_ref, o_ref, wb_ref):
    @pl.when(pl.program_id(1) == 0)
    def _():
        wb_ref[...] = w_ref[...].astype(BF16)

    o_ref[...] = jnp.dot(x_ref[...], wb_ref[...], preferred_element_type=F32).astype(o_ref.dtype)


def matmul(x, w, layer, *, tm, tn, col0=0, ncols=None, out_dtype=F32):
    M, K = x.shape
    ncols = w.shape[-1] if ncols is None else ncols
    assert M % tm == 0 and ncols % tn == 0 and col0 % tn == 0
    off = col0 // tn
    return pl.pallas_call(
        _mm_kernel,
        grid=(ncols // tn, M // tm),
        in_specs=[pl.BlockSpec((tm, K), lambda j, i: (i, 0)),
                  pl.BlockSpec((None, K, tn), lambda j, i: (layer, 0, j + off))],
        out_specs=pl.BlockSpec((tm, tn), lambda j, i: (i, j)),
        out_shape=jax.ShapeDtypeStruct((M, ncols), out_dtype),
        scratch_shapes=[pltpu.VMEM((K, tn), BF16)],
        compiler_params=_cparams("arbitrary", "arbitrary"),
        name="matmul",
    )(x, w)


SUBLANES = 8


def _layer_norm(v, g, b):
    mu = jnp.mean(v, axis=-1, keepdims=True)
    d = v - mu
    var = jnp.mean(d * d, axis=-1, keepdims=True)
    return d * lax.rsqrt(var + LN_EPS) * g + b


def _ln_kernel(alpha, x_ref, y_ref, g_ref, b_ref, o_ref, slab_ref):
    o_ref[...] = _layer_norm(alpha * x_ref[...] + y_ref[...], g_ref[...], b_ref[...])
    tm, D = o_ref.shape
    sub = D // LANES
    for r in range(tm // SUBLANES):
        for c in range(sub):
            slab_ref[pl.ds(r * SUBLANES * sub + c, SUBLANES, stride=sub), :] = (
                o_ref[r * SUBLANES:(r + 1) * SUBLANES, c * LANES:(c + 1) * LANES])


def residual_layer_norm(x, y, g, b, alpha, *, tm=256):
    M, D = x.shape
    spec = pl.BlockSpec((tm, D), lambda i: (i, 0))
    vec = pl.BlockSpec((1, D), lambda i: (0, 0))
    return pl.pallas_call(
        functools.partial(_ln_kernel, alpha),
        grid=(M // tm,),
        in_specs=[spec, spec, vec, vec],
        out_specs=[spec, pl.BlockSpec((tm * D // LANES, LANES), lambda i: (i, 0))],
        out_shape=[jax.ShapeDtypeStruct((M, D), F32), jax.ShapeDtypeStruct((M * D // LANES, LANES), F32)],
        compiler_params=_cparams("parallel"),
        name="residual_ln",
    )(x, y, g.reshape(1, D), b.reshape(1, D))


def _rope_kernel(scale, x_ref, cos_ref, sin_ref, o_ref):
    cos = cos_ref[...]
    sin = sin_ref[...]
    lane = lax.broadcasted_iota(jnp.int32, cos.shape, 1)
    first_half = (lane % NSA_HD) < (NSA_HD // 2)
    for j in range(x_ref.shape[1] // LANES):
        x = x_ref[:, j * LANES:(j + 1) * LANES]
        partner = jnp.where(first_half, pltpu.roll(x, LANES - NSA_HD // 2, 1), pltpu.roll(x, NSA_HD // 2, 1))
        o_ref[:, j * LANES:(j + 1) * LANES] = ((x * cos + partner * sin) * scale).astype(o_ref.dtype)


def rope(z, col0, width, cos, sin, *, scale=1.0, out_dtype=F32, tm=256):
    M = z.shape[0]
    assert col0 % width == 0 and width % LANES == 0
    return pl.pallas_call(
        functools.partial(_rope_kernel, scale),
        grid=(M // tm,),
        in_specs=[pl.BlockSpec((tm, width), lambda i: (i, col0 // width)),
                  pl.BlockSpec((tm, LANES), lambda i: (i, 0)),
                  pl.BlockSpec((tm, LANES), lambda i: (i, 0))],
        out_specs=pl.BlockSpec((tm, width), lambda i: (i, 0)),
        out_shape=jax.ShapeDtypeStruct((M, width), out_dtype),
        compiler_params=_cparams("parallel"),
        name="rope",
    )(z, cos, sin)


def rope_tables(pos):
    half = NSA_HD // 2
    inv = jnp.power(ROPE_THETA, -jnp.arange(half, dtype=F32) / half)
    ang = pos.astype(F32)[:, None] * inv[None, :]
    cos, sin = jnp.cos(ang), jnp.sin(ang)
    reps = LANES // NSA_HD
    return jnp.tile(jnp.concatenate([cos, cos], -1), (1, reps)), jnp.tile(jnp.concatenate([-sin, sin], -1), (1, reps))


HG_CHUNK = 64
HG_SUB = 16
HG_BLOCK = 256


def _split3(x):
    hi = x.astype(BF16)
    r1 = x - hi.astype(F32)
    mid = r1.astype(BF16)
    lo = (r1 - mid.astype(F32)).astype(BF16)
    return hi, mid, lo


def _log_forget(fa, loglb, log1mlb):
    log_sig = jnp.minimum(fa, 0.0) - jnp.log(1.0 + jnp.exp(-jnp.abs(fa)))
    b = log1mlb + log_sig
    mx = jnp.maximum(loglb, b)
    return mx + jnp.log(1.0 + jnp.exp(-jnp.abs(loglb - b)))


def _hgrn_gate_norm(o, ga, gn):
    o = o * lax.rsqrt(jnp.mean(o * o, axis=-1, keepdims=True) + 1e-6)
    return o * gn * _silu(ga)


def _hgrn_prompt_kernel(qa_ref, fa_ref, ia_ref, ga_ref, loglb_ref, log1mlb_ref, onemlb_ref, gn_ref,
                        o_ref, s_ref, st_ref):
    c = pl.program_id(1)

    @pl.when(c == 0)
    def _():
        st_ref[...] = jnp.zeros_like(st_ref)

    loglb, log1mlb, onemlb, gn = loglb_ref[...], log1mlb_ref[...], onemlb_ref[...], gn_ref[...]
    row = lax.broadcasted_iota(jnp.int32, (HG_CHUNK, HG_CHUNK), 0)
    col = lax.broadcasted_iota(jnp.int32, (HG_CHUNK, HG_CHUNK), 1)
    tril = (col <= row).astype(BF16)
    srow = lax.broadcasted_iota(jnp.int32, (HG_SUB, HG_SUB), 0)
    scol = lax.broadcasted_iota(jnp.int32, (HG_SUB, HG_SUB), 1)

    def chunk(ci, carry):
        r0 = pl.multiple_of(ci * HG_CHUNK, HG_CHUNK)
        rows = pl.ds(r0, HG_CHUNK)
        qa, fa, v = qa_ref[rows, :], fa_ref[rows, :], ia_ref[rows, :]
        q = _silu(qa)
        logf = _log_forget(fa, loglb, log1mlb)
        k = onemlb * _sigmoid(-fa)
        b = sum(jnp.dot(tril, t, preferred_element_type=F32) for t in _split3(logf))
        st = st_ref[...]
        vb = v.astype(BF16)
        o_inter = lax.dot_general((q * jnp.exp(b)).astype(BF16), st.astype(BF16), (((1,), (1,)), ((), ())),
                                  preferred_element_type=F32)
        for i in range(HG_CHUNK // HG_SUB):
            lo = i * HG_SUB
            qi, bi, ki = q[lo:lo + HG_SUB], b[lo:lo + HG_SUB], k[lo:lo + HG_SUB]
            o = o_inter[lo:lo + HG_SUB]
            if i > 0:
                anchor = b[lo - 1:lo]
                qt = (qi * jnp.exp(bi - anchor)).astype(BF16)
                kt = (k[:lo] * jnp.exp(anchor - b[:lo])).astype(BF16)
                a_off = lax.dot_general(qt, kt, (((1,), (1,)), ((), ())), preferred_element_type=F32)
                o = o + jnp.dot(a_off.astype(BF16), vb[:lo], preferred_element_type=F32)
            a_d = jnp.zeros((HG_SUB, HG_SUB), F32)
            for s in range(HG_SUB):
                p = qi * ki[s:s + 1] * jnp.exp(jnp.minimum(bi - bi[s:s + 1], 0.0))
                a_d = jnp.where(scol == s, jnp.sum(p, axis=-1, keepdims=True), a_d)
            a_d = jnp.where(scol <= srow, a_d, 0.0)
            o = o + jnp.dot(a_d.astype(BF16), vb[lo:lo + HG_SUB], preferred_element_type=F32)
            ga = ga_ref[pl.ds(r0 + lo, HG_SUB), :]
            o_ref[pl.ds(r0 + lo, HG_SUB), :] = _hgrn_gate_norm(o, ga, gn).astype(o_ref.dtype)
        b_last = b[HG_CHUNK - 1:HG_CHUNK]
        kt = (k * jnp.exp(b_last - b)).astype(BF16)
        st_ref[...] = st * jnp.exp(b_last) + lax.dot_general(vb, kt, (((0,), (0,)), ((), ())),
                                                             preferred_element_type=F32)
        return carry

    lax.fori_loop(0, HG_BLOCK // HG_CHUNK, chunk, 0, unroll=True)

    @pl.when(c == pl.num_programs(1) - 1)
    def _():
        s_ref[...] = st_ref[...].T


def hgrn_prompt(z, L, loglb, log1mlb, onemlb, gnorm):
    blk = lambda part: pl.BlockSpec((HG_BLOCK, A_DK), lambda h, c: (c, part * A_HEADS + h))
    vec = pl.BlockSpec((1, A_DK), lambda h, c: (0, h))
    return pl.pallas_call(
        _hgrn_prompt_kernel,
        grid=(A_HEADS, L // HG_BLOCK),
        in_specs=[blk(0), blk(1), blk(2), blk(3), vec, vec, vec, vec],
        out_specs=[pl.BlockSpec((HG_BLOCK, A_DK), lambda h, c: (c, h)),
                   pl.BlockSpec((None, A_DK, A_DK), lambda h, c: (h, 0, 0))],
        out_shape=[jax.ShapeDtypeStruct((L, A_HEADS * A_DK), BF16),
                   jax.ShapeDtypeStruct((A_HEADS, A_DK, A_DK), F32)],
        scratch_shapes=[pltpu.VMEM((A_DK, A_DK), F32)],
        compiler_params=_cparams("arbitrary", "arbitrary"),
        name="hgrn_prompt",
    )(z, z, z, z, loglb, log1mlb, onemlb, gnorm)


CMP_PAGES = 32
CMP_ROWS = CMP_PAGES * PAGE_SIZE
CMP_OUT = CMP_ROWS // CMP_STRIDE
CMP_W = NSA_KV * NSA_HD
CMP_PLANES = 2 * CMP_W // LANES
GROUPS_PER_PLANE = LANES // NSA_HD


def _compress_copies(pt_ref, cache_ref, tbuf, sem, layer, bb, cc, slot, n_pages):
    nxt = jnp.minimum((cc + 1) * CMP_PAGES, n_pages - 1)
    pages = [pt_ref[bb, cc * CMP_PAGES + j] for j in range(CMP_PAGES)] + [pt_ref[bb, nxt]]
    return [pltpu.make_async_copy(cache_ref.at[layer, pg, pl.ds(0, 2)], tbuf.at[slot, j], sem.at[slot])
            for j, pg in enumerate(pages)]


def _compress_kernel(layer, pt_ref, cache_ref, w1_hbm, pe_ref, w2_ref, kc_ref, vc_ref, tbuf, xbuf, w1buf, xcat, sem,
                     wsem):
    b, c = pl.program_id(0), pl.program_id(1)
    nb, nc = pl.num_programs(0), pl.num_programs(1)
    step = b * nc + c
    slot = step % 2
    n_pages = nc * CMP_PAGES
    mk = functools.partial(_compress_copies, pt_ref, cache_ref, tbuf, sem, layer)

    @pl.when(step == 0)
    def _():
        wcopy = pltpu.make_async_copy(w1_hbm, w1buf, wsem)
        wcopy.start()
        for cp in mk(b, c, slot, n_pages):
            cp.start()
        wcopy.wait()

    @pl.when(step + 1 < nb * nc)
    def _():
        last_c = c == nc - 1
        for cp in mk(jnp.where(last_c, b + 1, b), jnp.where(last_c, 0, c + 1), 1 - slot, n_pages):
            cp.start()

    for cp in mk(b, c, slot, n_pages):
        cp.wait()

    planes_per_out = CMP_PLANES // 2
    for j in range(CMP_PAGES + 1):
        n_rows = PAGE_SIZE if j < CMP_PAGES else CMP_STRIDE
        for t in range(2):
            for q in range(planes_per_out):
                tiles = tbuf[slot, j, t, q * GROUPS_PER_PLANE:(q + 1) * GROUPS_PER_PLANE]
                xbuf[t * planes_per_out + q, j * PAGE_SIZE:j * PAGE_SIZE + n_rows, :] = (
                    tiles.reshape(LANES, PAGE_SIZE).T[:n_rows])

    for t, o_ref in enumerate((kc_ref, vc_ref)):
        for q in range(planes_per_out):
            xs = xbuf.at[t * planes_per_out + q]
            for p in range(CMP_BLOCK):
                x = xs[pl.ds(p, CMP_OUT, stride=CMP_STRIDE), :] + pe_ref[t, p]
                xcat[:, p * LANES:(p + 1) * LANES] = x.astype(BF16)
            h = jnp.dot(xcat[...], w1buf[t], preferred_element_type=F32)
            o_ref[:, q * LANES:(q + 1) * LANES] = jnp.dot(_silu(h).astype(BF16), w2_ref[t],
                                                          preferred_element_type=F32)


def compress_weights(pe, w1, w2):
    eye = jnp.eye(GROUPS_PER_PLANE, dtype=F32)
    w1r = w1.reshape(2, CMP_BLOCK, NSA_HD, CMP_HIDDEN)
    w1bd = jnp.einsum('gh,tpdn->tpgdhn', eye, w1r).reshape(2, CMP_BLOCK * LANES, GROUPS_PER_PLANE * CMP_HIDDEN)
    w2bd = jnp.einsum('gh,tnd->tgnhd', eye, w2).reshape(2, GROUPS_PER_PLANE * CMP_HIDDEN, LANES)
    pet = jnp.tile(pe, (1, 1, GROUPS_PER_PLANE)).reshape(2, CMP_BLOCK, 1, LANES)
    return w1bd.astype(BF16), pet, w2bd.astype(BF16)


def compress_paged(cache, layer, page_table, w1bd, pet, w2bd):
    B, n_pages = page_table.shape
    assert n_pages % CMP_PAGES == 0
    nc = n_pages // CMP_PAGES
    out = jax.ShapeDtypeStruct((B, nc * CMP_OUT, CMP_W), F32)
    ospec = pl.BlockSpec((None, CMP_OUT, CMP_W), lambda b, c, pt: (b, c, 0))
    return pl.pallas_call(
        functools.partial(_compress_kernel, layer),
        grid_spec=pltpu.PrefetchScalarGridSpec(
            num_scalar_prefetch=1,
            grid=(B, nc),
            in_specs=[pl.BlockSpec(memory_space=pl.ANY),
                      pl.BlockSpec(memory_space=pl.ANY),
                      pl.BlockSpec(pet.shape, lambda b, c, pt: (0, 0, 0, 0)),
                      pl.BlockSpec(w2bd.shape, lambda b, c, pt: (0, 0, 0))],
            out_specs=[ospec, ospec],
            scratch_shapes=[pltpu.VMEM((2, CMP_PAGES + 1, 2, NSA_KV, NSA_HD, PAGE_SIZE), F32),
                            pltpu.VMEM((CMP_PLANES, CMP_ROWS + CMP_STRIDE, LANES), F32),
                            pltpu.VMEM(w1bd.shape, BF16),
                            pltpu.VMEM((CMP_OUT, CMP_BLOCK * LANES), BF16),
                            pltpu.SemaphoreType.DMA((2,)),
                            pltpu.SemaphoreType.DMA(())]),
        out_shape=[out, out],
        compiler_params=_cparams("arbitrary", "arbitrary"),
        name="nsa_compress",
    )(page_table, cache, w1bd, pet, w2bd)


NSA_TQ = 256
NSA_TK = 512
SEL_PER_CMP = SEL_BLOCK // CMP_STRIDE
NT_DIMS = (((1,), (1,)), ((), ()))


def _masked_softmax(s, mask):
    s = jnp.where(mask, s, NEG)
    e = jnp.exp(s - jnp.max(s, axis=-1, keepdims=True)) * mask.astype(F32)
    d = jnp.sum(e, axis=-1, keepdims=True)
    return e / jnp.where(d > 0, d, 1.0)


def _pool_matrix(n_cmp, n_sel):
    i = lax.broadcasted_iota(jnp.int32, (n_cmp, n_sel), 0)
    j = lax.broadcasted_iota(jnp.int32, (n_cmp, n_sel), 1)
    lo = SEL_PER_CMP * j - (CMP_BLOCK // CMP_STRIDE - 1)
    return ((i >= lo) & (i <= SEL_PER_CMP * j + SEL_PER_CMP - 1)).astype(BF16)


def _importance(psum, pool):
    return sum(jnp.dot(t, pool, preferred_element_type=F32) for t in _split3(psum))


def _top_k_mask(score, k):
    lane = lax.broadcasted_iota(jnp.int32, score.shape, 1).astype(F32)
    sel = jnp.zeros(score.shape, F32)
    for _ in range(k):
        m = jnp.max(score, axis=-1, keepdims=True)
        idx = jnp.min(jnp.where(score == m, lane, float(score.shape[-1])), axis=-1, keepdims=True)
        hit = lane == idx
        sel = jnp.where(hit, 1.0, sel)
        score = jnp.where(hit, -jnp.inf, score)
    return sel


def _nsa_prompt_kernel(q_ref, qr_ref, kc_ref, vc_ref, ks_ref, vs_ref, kw_ref, vw_ref, gate_ref, o_ref,
                       sel_ref, m_ref, l_ref, acc_ref, oc_ref, ow_ref):
    tq, tk = NSA_TQ, NSA_TK
    s0 = pl.program_id(1) * tq
    tpos = s0 + lax.broadcasted_iota(jnp.int32, (tq, 1), 0)
    n_cmp = kc_ref.shape[0]
    n_sel = sel_ref.shape[1]

    cidx = lax.broadcasted_iota(jnp.int32, (tq, n_cmp), 1)
    cmask = (cidx * CMP_STRIDE + (CMP_BLOCK - 1) <= tpos) & (cidx < n_cmp - 1)
    psum = jnp.zeros((tq, n_cmp), F32)
    for r in range(NSA_REP):
        s = lax.dot_general(q_ref[r], kc_ref[...], NT_DIMS, preferred_element_type=F32)
        p = _masked_softmax(s, cmask)
        oc_ref[r] = jnp.dot(p.astype(BF16), vc_ref[...], preferred_element_type=F32)
        psum = psum + p

    imp = _importance(psum, _pool_matrix(n_cmp, n_sel))
    blk = lax.broadcasted_iota(jnp.int32, (tq, n_sel), 1)
    valid = blk * SEL_BLOCK <= tpos
    forced = (blk == 0) | (blk == tpos // SEL_BLOCK)
    score = jnp.where(forced, jnp.inf, jnp.where(valid, imp, -jnp.inf))
    sel_ref[...] = _top_k_mask(score, N_SEL).astype(BF16)

    band = WINDOW + tq
    w0 = pl.multiple_of(jnp.maximum(s0 - WINDOW, 0), tq)
    wpos = w0 + lax.broadcasted_iota(jnp.int32, (tq, band), 1)
    wmask = (wpos <= tpos) & (wpos > tpos - WINDOW)
    kw = kw_ref[pl.ds(w0, band), :]
    vw = vw_ref[pl.ds(w0, band), :]
    for r in range(NSA_REP):
        s = lax.dot_general(qr_ref[r], kw, NT_DIMS, preferred_element_type=F32)
        ow_ref[r] = jnp.dot(_masked_softmax(s, wmask).astype(BF16), vw, preferred_element_type=F32)

    m_ref[...] = jnp.full(m_ref.shape, NEG, F32)
    l_ref[...] = jnp.zeros(l_ref.shape, F32)
    acc_ref[...] = jnp.zeros(acc_ref.shape, F32)

    def sweep(kt, carry):
        k0 = pl.multiple_of(kt * tk, tk)
        jj = lax.broadcasted_iota(jnp.int32, (n_sel, tk), 0)
        ss = lax.broadcasted_iota(jnp.int32, (n_sel, tk), 1)
        expand = (jj == kt * (tk // SEL_BLOCK) + ss // SEL_BLOCK).astype(BF16)
        chosen = jnp.dot(sel_ref[...], expand, preferred_element_type=F32)
        kpos = k0 + lax.broadcasted_iota(jnp.int32, (tq, tk), 1)
        allowed = (chosen > 0.5) & (kpos <= tpos)
        ks = ks_ref[pl.ds(k0, tk), :]
        vs = vs_ref[pl.ds(k0, tk), :]
        for r in range(NSA_REP):
            s = lax.dot_general(qr_ref[r], ks, NT_DIMS, preferred_element_type=F32)
            s = jnp.where(allowed, s, NEG)
            m_old = m_ref[r]
            m_new = jnp.maximum(m_old, jnp.max(s, axis=-1, keepdims=True))
            p = jnp.exp(s - m_new)
            alpha = jnp.exp(m_old - m_new)
            l_ref[r] = alpha * l_ref[r] + jnp.sum(p, axis=-1, keepdims=True)
            acc_ref[r] = alpha * acc_ref[r] + jnp.dot(p.astype(BF16), vs, preferred_element_type=F32)
            m_ref[r] = m_new
        return carry

    lax.fori_loop(0, (s0 + tq + tk - 1) // tk, sweep, 0)

    gates = _sigmoid(gate_ref[...])
    for r in range(NSA_REP):
        o_s = acc_ref[r] / l_ref[r]
        o = (gates[:, 3 * r:3 * r + 1] * oc_ref[r] + gates[:, 3 * r + 1:3 * r + 2] * o_s
             + gates[:, 3 * r + 2:3 * r + 3] * ow_ref[r])
        o_ref[:, r * NSA_HD:(r + 1) * NSA_HD] = o.astype(o_ref.dtype)


def nsa_prompt(q, qr, kc, vc, ks, vs, kw, vw, gates):
    L = q.shape[1]
    n_cmp = kc.shape[1]
    n_sel = L // SEL_BLOCK
    assert L % NSA_TK == 0 and L >= WINDOW + NSA_TQ
    qspec = pl.BlockSpec((NSA_REP, NSA_TQ, NSA_HD), lambda g, i: (g, i, 0))
    cspec = pl.BlockSpec((None, n_cmp, NSA_HD), lambda g, i: (g, 0, 0))
    kspec = pl.BlockSpec((None, L, NSA_HD), lambda g, i: (g, 0, 0))
    per_head = lambda w: pltpu.VMEM((NSA_REP, NSA_TQ, w), F32)
    return pl.pallas_call(
        _nsa_prompt_kernel,
        grid=(NSA_KV, L // NSA_TQ),
        in_specs=[qspec, qspec, cspec, cspec, kspec, kspec, kspec, kspec,
                  pl.BlockSpec((None, NSA_TQ, LANES), lambda g, i: (g, i, 0))],
        out_specs=pl.BlockSpec((NSA_TQ, NSA_REP * NSA_HD), lambda g, i: (i, g)),
        out_shape=jax.ShapeDtypeStruct((L, NSA_HEADS * NSA_HD), BF16),
        scratch_shapes=[pltpu.VMEM((NSA_TQ, n_sel), BF16),
                        per_head(1), per_head(1),
                        per_head(NSA_HD), per_head(NSA_HD), per_head(NSA_HD)],
        compiler_params=_cparams("parallel", "arbitrary"),
        name="nsa_prompt",
    )(q, qr, kc, vc, ks, vs, kw, vw, gates)


def _top_k_indices(score, k):
    lane = lax.broadcasted_iota(jnp.int32, score.shape, 1).astype(F32)
    out = jnp.zeros(score.shape, F32)
    for t in range(k):
        m = jnp.max(score, axis=-1, keepdims=True)
        idx = jnp.min(jnp.where(score == m, lane, float(score.shape[-1])), axis=-1, keepdims=True)
        out = jnp.where(lane == float(t), idx, out)
        score = jnp.where(lane == idx, -jnp.inf, score)
    return out.astype(jnp.int32)


def _nsa_sample_select_kernel(q_pos, q_ref, kc_ref, vc_ref, oc_ref, idx_ref):
    n_cmp = kc_ref.shape[0]
    n_sel = idx_ref.shape[-1]
    q = q_ref[...]
    head_group = lax.broadcasted_iota(jnp.int32, (NSA_HEADS, 1), 0) // NSA_REP
    cidx = lax.broadcasted_iota(jnp.int32, (NSA_HEADS, n_cmp), 1)
    cmask = (cidx * CMP_STRIDE + (CMP_BLOCK - 1) <= q_pos) & (cidx < n_cmp - 1)
    grp_row = lax.broadcasted_iota(jnp.int32, (NSA_KV, 1), 0)
    oc = jnp.zeros((NSA_HEADS, NSA_HD), F32)
    psum = jnp.zeros((NSA_KV, n_cmp), F32)
    for g in range(NSA_KV):
        cols = slice(g * NSA_HD, (g + 1) * NSA_HD)
        s = lax.dot_general(q, kc_ref[:, cols].astype(BF16), NT_DIMS, preferred_element_type=F32)
        p = _masked_softmax(s, cmask)
        o = jnp.dot(p.astype(BF16), vc_ref[:, cols].astype(BF16), preferred_element_type=F32)
        mine = head_group == g
        oc = jnp.where(mine, o, oc)
        pg = jnp.sum(jnp.where(mine, p, 0.0), axis=0, keepdims=True)
        psum = jnp.where(grp_row == g, pg, psum)
    oc_ref[...] = oc
    imp = _importance(psum, _pool_matrix(n_cmp, n_sel))
    blk = lax.broadcasted_iota(jnp.int32, (NSA_KV, n_sel), 1)
    score = jnp.where(blk == 0, jnp.inf, imp)
    idx_ref[...] = _top_k_indices(score, N_SEL - 1)


def nsa_sample_select(q, kc, vc, q_pos, n_sel):
    B, n_cmp = kc.shape[0], kc.shape[1]
    cspec = pl.BlockSpec((None, n_cmp, CMP_W), lambda b: (b, 0, 0))
    hspec = pl.BlockSpec((None, NSA_HEADS, NSA_HD), lambda b: (b, 0, 0))
    return pl.pallas_call(
        functools.partial(_nsa_sample_select_kernel, q_pos),
        grid=(B,),
        in_specs=[hspec, cspec, cspec],
        out_specs=[hspec, pl.BlockSpec((None, NSA_KV, n_sel), lambda b: (b, 0, 0))],
        out_shape=[jax.ShapeDtypeStruct((B, NSA_HEADS, NSA_HD), F32),
                   jax.ShapeDtypeStruct((B, NSA_KV, n_sel), jnp.int32)],
        compiler_params=_cparams("parallel"),
        name="nsa_sample_select",
    )(q, kc, vc)


N_PAST_SEL = N_SEL - 1
SEL_K, SEL_V = 2, 3
BLOCKS_PER_PAGE = PAGE_SIZE // SEL_BLOCK


def _sel_copies(pt_ref, idx_ref, cache_ref, kbuf, vbuf, sem, layer, n_pages, b, slot):
    out = []
    for g in range(NSA_KV):
        for t in range(N_PAST_SEL):
            blk = idx_ref[(b * NSA_KV + g) * N_SEL + t]
            pg = pt_ref[b * n_pages + blk // BLOCKS_PER_PAGE]
            dst = pl.ds(t * PAGE_SIZE, PAGE_SIZE)
            out.append(pltpu.make_async_copy(cache_ref.at[layer, pg, SEL_K, g], kbuf.at[slot, g, :, dst], sem.at[slot]))
            out.append(pltpu.make_async_copy(cache_ref.at[layer, pg, SEL_V, g], vbuf.at[slot, g, :, dst], sem.at[slot]))
    return out


def _softmax_with_self(s, s_self, mask=None):
    if mask is not None:
        s = jnp.where(mask, s, NEG)
    m = jnp.maximum(jnp.max(s, axis=-1, keepdims=True), s_self)
    e = jnp.exp(s - m)
    if mask is not None:
        e = e * mask.astype(F32)
    e_self = jnp.exp(s_self - m)
    return e, e_self, jnp.sum(e, axis=-1, keepdims=True) + e_self


def _nsa_sample_attend_kernel(layer, n_pages, win_first, pt_ref, idx_ref, cache_ref, qr_ref, new_ref, win_ref, oc_ref,
                              gate_ref, o_ref, kbuf, vbuf, sem):
    b = pl.program_id(0)
    nb = pl.num_programs(0)
    slot = b % 2
    mk = functools.partial(_sel_copies, pt_ref, idx_ref, cache_ref, kbuf, vbuf, sem, layer, n_pages)

    @pl.when(b == 0)
    def _():
        for i, cp in enumerate(mk(b, slot)):
            cp.start(i % 2)

    @pl.when(b + 1 < nb)
    def _():
        for i, cp in enumerate(mk(b + 1, 1 - slot)):
            cp.start(i % 2)

    for cp in mk(b, slot):
        cp.wait()

    qr = qr_ref[...]
    qf = qr.astype(F32)
    head_group = lax.broadcasted_iota(jnp.int32, (NSA_HEADS, 1), 0) // NSA_REP
    gates = _sigmoid(gate_ref[...])
    n_win = win_ref.shape[-1]
    n_key = N_PAST_SEL * PAGE_SIZE
    wmask = lax.broadcasted_iota(jnp.int32, (NSA_HEADS, n_win), 1) >= win_first
    key = lax.broadcasted_iota(jnp.int32, (NSA_HEADS, n_key), 1)
    key_block = (key % PAGE_SIZE) // SEL_BLOCK
    out = jnp.zeros((NSA_HEADS, NSA_HD), F32)
    for g in range(NSA_KV):
        cols = slice(g * NSA_HD, (g + 1) * NSA_HD)
        ks_new, vs_new = new_ref[0:1, cols], new_ref[1:2, cols]
        kw_new, vw_new = new_ref[2:3, cols], new_ref[3:4, cols]
        want = jnp.zeros((NSA_HEADS, n_key), jnp.int32)
        for t in range(N_PAST_SEL):
            blk = idx_ref[(b * NSA_KV + g) * N_SEL + t]
            want = jnp.where(key // PAGE_SIZE == t, blk % BLOCKS_PER_PAGE, want)
        s = jnp.dot(qr, kbuf[slot, g].astype(BF16), preferred_element_type=F32)
        e, e_self, d = _softmax_with_self(s, jnp.sum(qf * ks_new, axis=-1, keepdims=True), key_block == want)
        o_s = (lax.dot_general(e.astype(BF16), vbuf[slot, g].astype(BF16), NT_DIMS, preferred_element_type=F32)
               + e_self * vs_new) / d
        s = jnp.dot(qr, win_ref[0, g].astype(BF16), preferred_element_type=F32)
        e, e_self, d = _softmax_with_self(s, jnp.sum(qf * kw_new, axis=-1, keepdims=True), wmask)
        o_w = (lax.dot_general(e.astype(BF16), win_ref[1, g].astype(BF16), NT_DIMS, preferred_element_type=F32)
               + e_self * vw_new) / d
        o = gates[:, 0:1] * oc_ref[...] + gates[:, 1:2] * o_s + gates[:, 2:3] * o_w
        out = jnp.where(head_group == g, o, out)
    o_ref[...] = out


def nsa_sample_attend(cache, layer, page_table, idx, qr, new_rows, win, o_cmp, gates, win_first):
    B, n_pages = page_table.shape
    n_win = win.shape[-1]
    hspec = lambda: pl.BlockSpec((None, NSA_HEADS, NSA_HD), lambda b, pt, ix: (b, 0, 0))
    return pl.pallas_call(
        functools.partial(_nsa_sample_attend_kernel, layer, n_pages, win_first),
        grid_spec=pltpu.PrefetchScalarGridSpec(
            num_scalar_prefetch=2,
            grid=(B,),
            in_specs=[pl.BlockSpec(memory_space=pl.ANY),
                      hspec(),
                      pl.BlockSpec((None, 4, CMP_W), lambda b, pt, ix: (b, 0, 0)),
                      pl.BlockSpec((None, None, 2, NSA_KV, NSA_HD, n_win), lambda b, pt, ix: (layer, b, 0, 0, 0, 0)),
                      hspec(),
                      pl.BlockSpec((None, NSA_HEADS, LANES), lambda b, pt, ix: (b, 0, 0))],
            out_specs=hspec(),
            scratch_shapes=[pltpu.VMEM((2, NSA_KV, NSA_HD, N_PAST_SEL * PAGE_SIZE), F32),
                            pltpu.VMEM((2, NSA_KV, NSA_HD, N_PAST_SEL * PAGE_SIZE), F32),
                            pltpu.SemaphoreType.DMA((2,))]),
        out_shape=jax.ShapeDtypeStruct((B, NSA_HEADS, NSA_HD), F32),
        compiler_params=_cparams("arbitrary"),
        name="nsa_sample_attend",
    )(page_table.reshape(-1), idx.reshape(-1), cache, qr, new_rows, win, o_cmp, gates)


def _merge_kernel(oa_ref, ob_ref, wa_ref, wb_ref, ma_ref, mb_ref, o_ref, wa_bf, wb_bf):
    @pl.when(pl.program_id(1) == 0)
    def _():
        wa_bf[...] = wa_ref[...].astype(BF16)
        wb_bf[...] = wb_ref[...].astype(BF16)

    ya = jnp.dot(oa_ref[...], wa_bf[...], preferred_element_type=F32)
    yb = jnp.dot(ob_ref[...], wb_bf[...], preferred_element_type=F32)
    o_ref[...] = (_sigmoid(ma_ref[...]) * ya + _sigmoid(mb_ref[...]) * yb).astype(o_ref.dtype)


def merge(o_a, o_b, w_a, w_b, zm, layer, *, tm, tn=512):
    M, K = o_a.shape
    D = w_a.shape[-1]
    xspec = pl.BlockSpec((tm, K), lambda j, i: (i, 0))
    wspec = pl.BlockSpec((None, K, tn), lambda j, i: (layer, 0, j))
    return pl.pallas_call(
        _merge_kernel,
        grid=(D // tn, M // tm),
        in_specs=[xspec, xspec, wspec, wspec,
                  pl.BlockSpec((tm, tn), lambda j, i: (i, j)),
                  pl.BlockSpec((tm, tn), lambda j, i: (i, j + D // tn))],
        out_specs=pl.BlockSpec((tm, tn), lambda j, i: (i, j)),
        out_shape=jax.ShapeDtypeStruct((M, D), BF16),
        scratch_shapes=[pltpu.VMEM((K, tn), BF16), pltpu.VMEM((K, tn), BF16)],
        compiler_params=_cparams("arbitrary", "arbitrary"),
        name="merge",
    )(o_a, o_b, w_a, w_b, zm, zm)


MOE_TILE = 256
ROUTE_TILE = 256


def _first_argmax(vals):
    best, idx = vals[0], jnp.zeros(vals[0].shape, F32)
    for e in range(1, len(vals)):
        better = vals[e] > best
        idx = jnp.where(better, float(e), idx)
        best = jnp.where(better, vals[e], best)
    return best, idx


def _router_kernel(x_ref, w_ref, b_ref, o_ref):
    xs = _split3(x_ref[...])
    ws = _split3(w_ref[...])
    terms = ((0, 0), (0, 1), (1, 0), (0, 2), (2, 0), (1, 1))
    logits = b_ref[...] + sum(lax.dot_general(ws[a], xs[c], NT_DIMS, preferred_element_type=F32) for a, c in terms)
    m = jnp.max(logits, axis=0, keepdims=True)
    e = jnp.exp(logits - m)
    probs = e / jnp.sum(e, axis=0, keepdims=True)
    p = [probs[i:i + 1] for i in range(N_EXPERTS)]
    gscore = []
    for g in range(N_GROUPS):
        mem = p[g * EXPERTS_PER_GROUP:(g + 1) * EXPERTS_PER_GROUP]
        pairs = [mem[a] + mem[c] for a in range(EXPERTS_PER_GROUP) for c in range(a + 1, EXPERTS_PER_GROUP)]
        gscore.append(functools.reduce(jnp.maximum, pairs))
    _, g_sel = _first_argmax(gscore)
    masked = [jnp.where(g_sel == float(i // EXPERTS_PER_GROUP), p[i], -1.0) for i in range(N_EXPERTS)]
    v1, i1 = _first_argmax(masked)
    masked2 = [jnp.where(i1 == float(i), -2.0, masked[i]) for i in range(N_EXPERTS)]
    v2, i2 = _first_argmax(masked2)
    tot = v1 + v2
    o_ref[...] = jnp.concatenate([i1, i2, v1 / tot, v2 / tot, jnp.zeros((4,) + i1.shape[1:], F32)], axis=0)


def moe_route(x, w_router, b_router):
    T, D = x.shape
    return pl.pallas_call(
        _router_kernel,
        grid=(T // ROUTE_TILE,),
        in_specs=[pl.BlockSpec((ROUTE_TILE, D), lambda i: (i, 0)),
                  pl.BlockSpec((N_EXPERTS, D), lambda i: (0, 0)),
                  pl.BlockSpec((N_EXPERTS, 1), lambda i: (0, 0))],
        out_specs=pl.BlockSpec((8, ROUTE_TILE), lambda i: (0, i)),
        out_shape=jax.ShapeDtypeStruct((8, T), F32),
        compiler_params=_cparams("parallel"),
        name="moe_route",
    )(x, w_router.T, b_router.reshape(N_EXPERTS, 1))


def _start_row_copies(n, hbm, rows_ref, base, buf, sub, sem, to_hbm):
    def start(j, priority):
        win = buf.at[pl.ds(pl.multiple_of(j * sub, sub), sub), :]
        row = hbm.at[pl.ds(pl.multiple_of(rows_ref[base + j] * sub, sub), sub), :]
        (pltpu.make_async_copy(win, row, sem) if to_hbm else pltpu.make_async_copy(row, win, sem)).start(priority)

    def body(i, carry):
        start(2 * i, 0)
        if isinstance(n, int) and n % 2 == 0:
            start(2 * i + 1, 1)
        else:
            pl.when(2 * i + 1 < n)(lambda: start(2 * i + 1, 1))
        return carry

    lax.fori_loop(0, (n + 1) // 2, body, 0)


def _wait_row_copies(n, hbm, buf, sub, sem, to_hbm):
    size = MOE_TILE
    while size:
        def wait(size=size):
            a, b = buf.at[pl.ds(0, size * sub), :], hbm.at[pl.ds(0, size * sub), :]
            (pltpu.make_async_copy(a, b, sem) if to_hbm else pltpu.make_async_copy(b, a, sem)).wait()

        if isinstance(n, int):
            if n & size:
                wait()
        else:
            pl.when((n & size) != 0)(wait)
        size //= 2


def _expert_ffn_kernel(src_ref, dst_ref, te_ref, nv_ref, nt_ref, xw_hbm, w1_ref, w3_ref, w2_ref, y_hbm,
                       xbuf, x2d, y2d, obuf, gsem, ssem):
    t = pl.program_id(0)
    nt = nt_ref[0]
    slot = t % 2
    xsub = xbuf.shape[1] // MOE_TILE
    ysub = obuf.shape[0] // MOE_TILE
    start_gather = lambda tile, sl: _start_row_copies(MOE_TILE, xw_hbm, src_ref, tile * MOE_TILE, xbuf.at[sl], xsub,
                                                      gsem.at[sl], False)
    wait_scatter = lambda tile: _wait_row_copies(nv_ref[tile], y_hbm, obuf, ysub, ssem, True)

    @pl.when((t == 0) & (nt > 0))
    def _():
        start_gather(t, slot)

    @pl.when(t + 1 < nt)
    def _():
        start_gather(t + 1, 1 - slot)

    @pl.when(t < nt)
    def _():
        _wait_row_copies(MOE_TILE, xw_hbm, xbuf.at[slot], xsub, gsem.at[slot], False)
        xs = xbuf.at[slot]
        for r in range(MOE_TILE // SUBLANES):
            for c in range(xsub):
                x2d[r * SUBLANES:(r + 1) * SUBLANES, c * LANES:(c + 1) * LANES] = (
                    xs[pl.ds(r * SUBLANES * xsub + c, SUBLANES, stride=xsub), :])
        x = x2d[...].astype(BF16)
        h = _silu(jnp.dot(x, w1_ref[...], preferred_element_type=F32)) * jnp.dot(x, w3_ref[...],
                                                                              preferred_element_type=F32)
        y2d[...] = jnp.dot(h.astype(BF16), w2_ref[...], preferred_element_type=F32)

        @pl.when(t > 0)
        def _():
            wait_scatter(t - 1)

        for r in range(MOE_TILE // SUBLANES):
            for c in range(ysub):
                obuf[pl.ds(r * SUBLANES * ysub + c, SUBLANES, stride=ysub), :] = (
                    y2d[r * SUBLANES:(r + 1) * SUBLANES, c * LANES:(c + 1) * LANES])
        _start_row_copies(nv_ref[t], y_hbm, dst_ref, t * MOE_TILE, obuf, ysub, ssem, True)

        @pl.when(t == nt - 1)
        def _():
            wait_scatter(t)


def expert_ffn(xw, src_token, dst_row, n_valid, n_rows_out, tile_expert, n_tiles_used, w1, w3, w2, layer):
    P = src_token.shape[0]
    D, F = w1.shape[2], w1.shape[3]
    xsub = ysub = D // LANES
    wspec = lambda a, c: pl.BlockSpec((None, None, a, c), lambda t, src, dst, te, nv, nt: (layer, te[t], 0, 0))
    return pl.pallas_call(
        _expert_ffn_kernel,
        grid_spec=pltpu.PrefetchScalarGridSpec(
            num_scalar_prefetch=5,
            grid=(P // MOE_TILE,),
            in_specs=[pl.BlockSpec(memory_space=pl.ANY), wspec(D, F), wspec(D, F), wspec(F, D)],
            out_specs=pl.BlockSpec(memory_space=pl.ANY),
            scratch_shapes=[pltpu.VMEM((2, MOE_TILE * xsub, LANES), F32),
                            pltpu.VMEM((MOE_TILE, xsub * LANES), F32),
                            pltpu.VMEM((MOE_TILE, D), F32),
                            pltpu.VMEM((MOE_TILE * ysub, LANES), F32),
                            pltpu.SemaphoreType.DMA((2,)),
                            pltpu.SemaphoreType.DMA(())]),
        out_shape=jax.ShapeDtypeStruct((n_rows_out * ysub, LANES), F32),
        compiler_params=_cparams("arbitrary"),
        name="expert_ffn",
    )(src_token, dst_row, tile_expert, n_valid, n_tiles_used, xw, w1, w3, w2)


def _combine_ln_kernel(alpha, x_ref, y1_ref, y2_ref, wt_ref, g_ref, b_ref, o_ref, ob_ref, y_scr):
    tm, D = o_ref.shape
    sub = D // LANES
    for r in range(tm // SUBLANES):
        rows = slice(r * SUBLANES, (r + 1) * SUBLANES)
        w1, w2 = wt_ref[rows, 0:1], wt_ref[rows, 1:2]
        for c in range(sub):
            pick = pl.ds(r * SUBLANES * sub + c, SUBLANES, stride=sub)
            y_scr[rows, c * LANES:(c + 1) * LANES] = w1 * y1_ref[pick, :] + w2 * y2_ref[pick, :]
    out = _layer_norm(alpha * x_ref[...] + y_scr[...], g_ref[...], b_ref[...])
    o_ref[...] = out
    ob_ref[...] = out.astype(BF16)


def moe_combine_layer_norm(x, ysel, wts, g, b, alpha, *, tm=256):
    T, D = x.shape
    nt = T // tm
    sub = D // LANES
    spec = pl.BlockSpec((tm, D), lambda i: (i, 0))
    vec = pl.BlockSpec((1, D), lambda i: (0, 0))
    return pl.pallas_call(
        functools.partial(_combine_ln_kernel, alpha),
        grid=(nt,),
        in_specs=[spec, pl.BlockSpec((tm * sub, LANES), lambda i: (i, 0)),
                  pl.BlockSpec((tm * sub, LANES), lambda i: (i + nt, 0)),
                  pl.BlockSpec((tm, 2), lambda i: (i, 0)), vec, vec],
        out_specs=[spec, spec],
        out_shape=[jax.ShapeDtypeStruct((T, D), F32), jax.ShapeDtypeStruct((T, D), BF16)],
        scratch_shapes=[pltpu.VMEM((tm, D), F32)],
        compiler_params=_cparams("parallel"),
        name="moe_combine_ln",
    )(x, ysel, ysel, wts, g.reshape(1, D), b.reshape(1, D))


def moe_dispatch_plan(e1, e2, n_tiles):
    T = e1.shape[0]
    ea = jnp.concatenate([e1, e2])
    onehot = (ea[:, None] == jnp.arange(N_EXPERTS)[None, :]).astype(jnp.int32)
    rank = jnp.take_along_axis(jnp.cumsum(onehot, axis=0) - onehot, ea[:, None], axis=1)[:, 0]
    counts = onehot.sum(0)
    tiles = (counts + MOE_TILE - 1) // MOE_TILE
    tile_end = jnp.cumsum(tiles)
    dest = ((tile_end - tiles) * MOE_TILE)[ea] + rank
    n_slots = n_tiles * MOE_TILE
    src_token = jnp.zeros((n_slots,), jnp.int32).at[dest].set(jnp.tile(jnp.arange(T, dtype=jnp.int32), 2))
    dst_row = jnp.zeros((n_slots,), jnp.int32).at[dest].set(jnp.arange(2 * T, dtype=jnp.int32))
    tile_idx = jnp.arange(n_tiles)
    tile_expert = jnp.minimum((tile_end[None, :] <= tile_idx[:, None]).sum(-1), N_EXPERTS - 1)
    first_tile = (tile_end - tiles)[tile_expert]
    n_valid = jnp.clip(counts[tile_expert] - (tile_idx - first_tile) * MOE_TILE, 0, MOE_TILE)
    return (src_token, dst_row, tile_expert.astype(jnp.int32), n_valid.astype(jnp.int32),
            tile_end[-1:].astype(jnp.int32))


def moe_block(x, x_slabs, w_router, b_router, w1, w3, w2, layer, g, b, alpha):
    T, D = x.shape
    r = moe_route(x, w_router, b_router)
    e1, e2 = r[0].astype(jnp.int32), r[1].astype(jnp.int32)
    n_tiles = 2 * T // MOE_TILE + N_EXPERTS
    src_token, dst_row, tile_expert, n_valid, n_used = moe_dispatch_plan(e1, e2, n_tiles)
    ysel = expert_ffn(x_slabs, src_token, dst_row, n_valid, 2 * T, tile_expert, n_used, w1, w3, w2, layer)
    return moe_combine_layer_norm(x, ysel, r[2:4].T, g, b, alpha)


def _hgrn_sample_kernel(qt_ref, ft_ref, lbt_ref, onemt_ref, ia_ref, ga_ref, gn_ref, s0_ref, o_ref, s_ref):
    qa = qt_ref[...]
    fa = ft_ref[...]
    q_all = _silu(qa)
    sig = _sigmoid(fa)
    f_all = lbt_ref[...] + onemt_ref[...] * sig
    k_all = onemt_ref[...] * _sigmoid(-fa)
    for h in range(A_HEADS):
        cols = slice(h * A_DK, (h + 1) * A_DK)
        v = ia_ref[:, cols]
        s_new = f_all[:, h:h + 1] * s0_ref[h] + k_all[:, h:h + 1] * v
        s_ref[h] = s_new
        o = jnp.sum(s_new * q_all[:, h:h + 1], axis=0, keepdims=True)
        o_ref[:, cols] = _hgrn_gate_norm(o, ga_ref[:, cols], gn_ref[:, cols]).astype(o_ref.dtype)


def hgrn_sample(qa_t, fa_t, lb_t, onem_t, ia, ga, gnorm, state, layer):
    B = qa_t.shape[0]
    W = A_HEADS * A_DK
    col = pl.BlockSpec((None, A_DK, A_HEADS), lambda b: (b, 0, 0))
    par = pl.BlockSpec((A_DK, A_HEADS), lambda b: (0, 0))
    row = pl.BlockSpec((None, 1, W), lambda b: (b, 0, 0))
    return pl.pallas_call(
        _hgrn_sample_kernel,
        grid=(B,),
        in_specs=[col, col, par, par, row, row, pl.BlockSpec((1, W), lambda b: (0, 0)),
                  pl.BlockSpec((None, None, A_HEADS, A_DK, A_DK), lambda b: (layer, b, 0, 0, 0))],
        out_specs=[row, pl.BlockSpec((None, A_HEADS, A_DK, A_DK), lambda b: (b, 0, 0, 0))],
        out_shape=[jax.ShapeDtypeStruct((B, 1, W), BF16), jax.ShapeDtypeStruct((B, A_HEADS, A_DK, A_DK), F32)],
        compiler_params=_cparams("parallel"),
        name="hgrn_sample",
    )(qa_t, fa_t, lb_t, onem_t, ia, ga, gnorm, state)


MM_TILE_M = 768
MM_TILE_N = 512
W_A = A_HEADS * A_DK
W_Q = NSA_HEADS * NSA_HD
COL_QB = 4 * W_A
COL_KV = COL_QB + W_Q
COL_GB = COL_KV + 6 * CMP_W
COL_MA = COL_GB + 3 * NSA_HEADS


def _pad_lanes(a):
    return jnp.pad(a, [(0, 0)] * (a.ndim - 1) + [(0, LANES - a.shape[-1])])


def token_minor(a):
    return jnp.transpose(a, (0, 1, 3, 4, 5, 2))

def kernel(x_prompt, x_sample, cache_nsa_kv, cache_nsa_win, state_hgrn, page_table, w_in, lb_raw, gnorm_a, cmp_pe, cmp_w1, cmp_w2, w_branch_a, w_branch_b, w_out, ln1_g, ln1_b, ln2_g, ln2_b, w_router, b_router, w_e1, w_e3, w_e2):
    depth, D = w_in.shape[0], w_in.shape[1]
    L, B = x_prompt.shape[1], x_sample.shape[0]
    assert x_prompt.shape[0] == 1 and x_sample.shape[1] == 1
    n_pool = cache_nsa_kv.shape[1]
    past_len = page_table.shape[1] * PAGE_SIZE
    win_rows = cache_nsa_win.shape[2]
    T = L + B
    M = -(-T // MM_TILE_M) * MM_TILE_M
    alpha = (2 * depth) ** 0.25
    prompt, sample = slice(0, L), slice(L, T)

    lbs = jnp.cumsum(jax.nn.softmax(lb_raw.astype(F32), axis=0), axis=0)
    lbs = lbs - lbs[0:1]
    pos = jnp.concatenate([jnp.arange(L), jnp.full((B,), past_len), jnp.zeros((M - T,), jnp.int32)])
    cos, sin = rope_tables(pos)
    ones, zeros = jnp.ones_like(cos), jnp.zeros_like(cos)
    cache = token_minor(cache_nsa_kv)
    win_cache = token_minor(cache_nsa_win)
    w1b, w3b, w2b = w_e1.astype(BF16), w_e3.astype(BF16), w_e2.astype(BF16)
    scale = NSA_HD ** -0.5

    x = jnp.concatenate([x_prompt[0], x_sample[:, 0], jnp.zeros((M - T, D), F32)])
    xb = x.astype(BF16)
    heads_major = lambda a, h: a.reshape(a.shape[0], h, NSA_HD).transpose(1, 0, 2)
    col_major = lambda a: a.reshape(B, A_HEADS, A_DK).transpose(0, 2, 1)
    outs = [[] for _ in range(6)]
    for l in range(depth):
        lb = lbs[l]
        mm = functools.partial(matmul, tm=MM_TILE_M, tn=MM_TILE_N)
        z = mm(xb, w_in, l, col0=0, ncols=COL_GB)
        zg = matmul(xb, _pad_lanes(w_in[l, :, COL_GB:COL_MA])[None], 0, tm=MM_TILE_M, tn=LANES)
        zm = mm(xb, w_in[l, :, COL_MA:][None], 0)

        o_a_p, st_p = hgrn_prompt(z, L, jnp.log(lb)[None], jnp.log1p(-lb)[None], (1.0 - lb)[None], gnorm_a[l][None])
        zs = z[sample]
        o_a_s, st_s = hgrn_sample(col_major(zs[:, :W_A]), col_major(zs[:, W_A:2 * W_A]),
                                  lb.reshape(A_HEADS, A_DK).T, (1.0 - lb).reshape(A_HEADS, A_DK).T,
                                  zs[:, None, 2 * W_A:3 * W_A], zs[:, None, 3 * W_A:4 * W_A], gnorm_a[l][None],
                                  state_hgrn, l)
        o_a = jnp.concatenate([o_a_p, o_a_s[:, 0], jnp.zeros((M - T, W_A), BF16)])

        q_plain = rope(z, COL_QB, W_Q, ones, zeros, scale=scale, out_dtype=BF16)
        q_rot = rope(z, COL_QB, W_Q, cos, sin, scale=scale, out_dtype=BF16)
        ks_rot = rope(z, COL_KV + 2 * CMP_W, CMP_W, cos, sin)
        kw_rot = rope(z, COL_KV + 4 * CMP_W, CMP_W, cos, sin)
        v_sel = z[:, COL_KV + 3 * CMP_W:COL_KV + 4 * CMP_W]
        v_win = z[:, COL_KV + 5 * CMP_W:COL_KV + 6 * CMP_W]
        rows = jnp.concatenate([z[:, COL_KV:COL_KV + 2 * CMP_W], ks_rot, v_sel], axis=-1)
        win_new = jnp.concatenate([kw_rot, v_win], axis=-1)
        cw = compress_weights(cmp_pe[l], cmp_w1[l], cmp_w2[l])

        prompt_pages = token_minor(rows[prompt].reshape(1, L // PAGE_SIZE, PAGE_SIZE, 4, NSA_KV, NSA_HD))
        kc_p, vc_p = compress_paged(prompt_pages, 0, jnp.arange(L // PAGE_SIZE, dtype=jnp.int32)[None], *cw)
        grp = lambda a: heads_major(a, NSA_KV).astype(BF16)
        gates_p =_pad_lanes(zg[prompt, :3 * NSA_HEADS].reshape(L, NSA_KV, 3 * NSA_REP).transpose(1, 0, 2))
        o_b_p = nsa_prompt(heads_major(q_plain[prompt], NSA_HEADS), heads_major(q_rot[prompt], NSA_HEADS),
                           grp(kc_p[0]), grp(vc_p[0]), grp(ks_rot[prompt]), grp(v_sel[prompt]),
                           grp(kw_rot[prompt]), grp(v_win[prompt]), gates_p)

        kc_s, vc_s = compress_paged(cache, l, page_table, *cw)
        o_c_s, idx = nsa_sample_select(q_plain[sample].reshape(B, NSA_HEADS, NSA_HD), kc_s, vc_s,
                                       past_len, past_len // SEL_BLOCK)
        rs, ws = rows[sample], win_new[sample]
        new_rows = jnp.stack([rs[:, 2 * CMP_W:3 * CMP_W], rs[:, 3 * CMP_W:], ws[:, :CMP_W], ws[:, CMP_W:]], axis=1)
        gates_s = _pad_lanes(zg[sample, :3 * NSA_HEADS].reshape(B, NSA_HEADS, 3))
        o_b_s = nsa_sample_attend(cache, l, page_table, idx[:, :, :N_SEL], q_rot[sample].reshape(B, NSA_HEADS, NSA_HD),
                                  new_rows, win_cache, o_c_s, gates_s, win_rows - WINDOW + 1)
        o_b = jnp.concatenate([o_b_p, o_b_s.reshape(B, W_Q).astype(BF16), jnp.zeros((M - T, W_Q), BF16)])

        h = merge(o_a, o_b, w_branch_a, w_branch_b, zm, l, tm=MM_TILE_M)
        y = mm(h, w_out, l)
        x1, x1_slabs = residual_layer_norm(x, y, ln1_g[l], ln1_b[l], alpha)
        x, xb = moe_block(x1, x1_slabs, w_router, b_router, w1b, w3b, w2b, l, ln2_g[l], ln2_b[l], alpha)

        wbp = min(WINDOW, L)
        outs[0].append(rows[prompt].reshape(1, L, 4, NSA_KV, NSA_HD))
        outs[1].append(rs.reshape(B, 1, 4, NSA_KV, NSA_HD))
        outs[2].append(win_new[L - wbp:L].reshape(1, wbp, 2, NSA_KV, NSA_HD))
        outs[3].append(jnp.concatenate([cache_nsa_win[l][:, 1:], ws.reshape(B, 1, 2, NSA_KV, NSA_HD)], axis=1))
        outs[4].append(st_p[None])
        outs[5].append(st_s)
    return (x[prompt][None], x[sample][:, None]) + tuple(jnp.stack(o) for o in outs)
```
